```python
import jax, jax.numpy as jnp
from jax import lax
import numpy as np

D_MODEL = 1024
BATCH = 2
SEQ = 8192
DEPTH = 4

CHUNK = 128
A_GROUPS = 4
A_CH = 128
A_WIDTH = A_GROUPS * A_CH
LRU_BLOCKS = 8
LRU_BLOCK = 64
LRU_WIDTH = LRU_BLOCKS * LRU_BLOCK
CONV_W = 4
C_LRU = 8.0
HEAD_DIM = 64
NSA_HEADS = 8
NSA_KV = 2
NSA_GROUP = NSA_HEADS // NSA_KV
NSA_WIDTH = NSA_HEADS * HEAD_DIM
KV_WIDTH = NSA_KV * HEAD_DIM
CMP_LEN = 32
CMP_STRIDE = 16
CMP_HIDDEN = 128
SLC_BLOCK = 64
N_SELECT = 16
WINDOW = 512
Q_BLOCK = 128
N_GATES = NSA_HEADS * 3
N_MEM = 256
MEM_HEADS = 4
MEM_WIDTH = MEM_HEADS * HEAD_DIM

MIX_WIDTH = A_WIDTH + LRU_WIDTH + NSA_WIDTH + MEM_WIDTH
IN_SIZES = (A_WIDTH, A_WIDTH, A_WIDTH, LRU_WIDTH, LRU_WIDTH, NSA_WIDTH, NSA_WIDTH,
            KV_WIDTH, KV_WIDTH, KV_WIDTH, KV_WIDTH, KV_WIDTH, KV_WIDTH, N_GATES, MEM_WIDTH, MEM_WIDTH)
D_IN = sum(IN_SIZES)

DEEPNORM_ALPHA = (2 * DEPTH) ** 0.25
DEEPNORM_BETA = (8 * DEPTH) ** -0.25
LN_EPS = 1e-5
NEG_INF = -1e30
FORCE_BONUS = 1e4

kernel_name = 'hybrid_parallel_gmlp_rglru_nsa_mem'


def layer_norm(x, g, b):
    xf = x.astype(jnp.float32)
    mu = jnp.mean(xf, -1, keepdims=True)
    var = jnp.mean(jnp.square(xf - mu), -1, keepdims=True)
    y = (xf - mu) * lax.rsqrt(var + LN_EPS)
    return (y * g.astype(jnp.float32) + b.astype(jnp.float32)).astype(x.dtype)


def spatial_gating(u, v, ln_g, ln_b, w_s, b_s):
    bsz, s, _ = u.shape
    nc = s // CHUNK
    u = jax.nn.gelu(u).reshape(bsz, nc, CHUNK, A_GROUPS, A_CH)
    v = jax.nn.gelu(v).reshape(bsz, nc, CHUNK, A_GROUPS, A_CH)
    v = layer_norm(v, ln_g.reshape(A_GROUPS, A_CH), ln_b.reshape(A_GROUPS, A_CH))
    causal = jnp.tril(jnp.ones((CHUNK, CHUNK), dtype=bool))
    w = jnp.where(causal[None], w_s, 0)
    sv = jnp.einsum('gts,bnsgc->bntgc', w, v) + jnp.swapaxes(b_s, 0, 1)[:, :, None]
    return (u * sv).reshape(bsz, s, A_WIDTH)


def causal_depthwise_conv(x, w, b):
    s = x.shape[1]
    xp = jnp.pad(x, ((0, 0), (CONV_W - 1, 0), (0, 0)))
    y = b
    for k in range(CONV_W):
        y = y + xp[:, k:k + s] * w[k]
    return y


def _linear_combine(left, right):
    a1, b1 = left
    a2, b2 = right
    return a1 * a2, a2 * b1 + b2


def rglru_branch(xb, conv_w, conv_b, wa, ba, wx, bx, lam):
    bsz, s, _ = xb.shape
    xc = causal_depthwise_conv(xb, conv_w, conv_b)
    xh = xc.reshape(bsz, s, LRU_BLOCKS, LRU_BLOCK)
    r = jax.nn.sigmoid(jnp.einsum('bshi,hij->bshj', xh, wa).reshape(bsz, s, LRU_WIDTH) + ba)
    i = jax.nn.sigmoid(jnp.einsum('bshi,hij->bshj', xh, wx).reshape(bsz, s, LRU_WIDTH) + bx)
    log_a = -C_LRU * r.astype(jnp.float32) * jax.nn.softplus(-lam.astype(jnp.float32))
    a = jnp.exp(log_a)
    inp = jnp.sqrt(-jnp.expm1(2.0 * log_a)) * (i * xc).astype(jnp.float32)
    _, h = lax.associative_scan(_linear_combine, (a, inp), axis=1)
    return h.astype(xb.dtype)


def compress_blocks(k, pos, w1, w2):
    bsz, s, g, d = k.shape
    ncmp = (s - CMP_LEN) // CMP_STRIDE + 1
    idx = jnp.arange(ncmp)[:, None] * CMP_STRIDE + jnp.arange(CMP_LEN)[None, :]
    kb = k[:, idx] + pos[:, None, :]
    kb = jnp.transpose(kb, (0, 1, 3, 2, 4)).reshape(bsz, ncmp, g, CMP_LEN * d)
    return jax.nn.gelu(kb @ w1) @ w2


def _masked_softmax(scores, valid):
    scores = jnp.where(valid, scores, NEG_INF)
    return jax.nn.softmax(scores, axis=-1) * valid


def nsa_branch(q, kc_raw, vc_raw, ks_raw, vs_raw, kw_raw, vw_raw, g_logits, cmp_pos, cmp_w1, cmp_w2):
    bsz, s, _ = q.shape
    scale = HEAD_DIM ** -0.5
    q = q.reshape(bsz, s, NSA_KV, NSA_GROUP, HEAD_DIM)
    kvshape = lambda t: t.reshape(bsz, s, NSA_KV, HEAD_DIM)
    kc = compress_blocks(kvshape(kc_raw), cmp_pos[0], cmp_w1[0], cmp_w2[0])
    vc = compress_blocks(kvshape(vc_raw), cmp_pos[1], cmp_w1[1], cmp_w2[1])
    ncmp = kc.shape[1]
    nslc = s // SLC_BLOCK
    n_sel = min(N_SELECT, nslc)
    ks_blk = jnp.transpose(kvshape(ks_raw).reshape(bsz, nslc, SLC_BLOCK, NSA_KV, HEAD_DIM), (0, 3, 1, 2, 4))
    vs_blk = jnp.transpose(kvshape(vs_raw).reshape(bsz, nslc, SLC_BLOCK, NSA_KV, HEAD_DIM), (0, 3, 1, 2, 4))
    kw_pad = jnp.pad(kvshape(kw_raw), ((0, 0), (WINDOW, 0), (0, 0), (0, 0)))
    vw_pad = jnp.pad(kvshape(vw_raw), ((0, 0), (WINDOW, 0), (0, 0), (0, 0)))
    ci = jnp.arange(ncmp)[:, None] * CMP_STRIDE
    sj = jnp.arange(nslc)[None, :] * SLC_BLOCK
    overlap = ((ci < sj + SLC_BLOCK) & (ci + CMP_LEN > sj)).astype(jnp.float32)
    cmp_end = jnp.arange(ncmp) * CMP_STRIDE + CMP_LEN - 1
    blk = jnp.arange(nslc)
    bi = jnp.arange(bsz)[:, None, None, None]
    gi = jnp.arange(NSA_KV)[None, :, None, None]

    def query_block(s0):
        t = s0 + jnp.arange(Q_BLOCK)
        qb = lax.dynamic_slice_in_dim(q, s0, Q_BLOCK, axis=1)
        sc = jnp.einsum('btghd,bngd->bghtn', qb, kc).astype(jnp.float32) * scale
        valid_c = cmp_end[None, :] <= t[:, None]
        p_c = _masked_softmax(sc, valid_c)
        o_c = jnp.einsum('bghtn,bngd->btghd', p_c.astype(vc.dtype), vc)
        imp = jnp.einsum('bghtn,nj->bgtj', p_c, overlap)
        cur = t // SLC_BLOCK
        future = blk[None, :] > cur[:, None]
        forced = (blk[None, :] == 0) | (blk[None, :] == cur[:, None]) | (blk[None, :] == cur[:, None] - 1)
        imp = jnp.where(future, -1.0, imp + FORCE_BONUS * forced.astype(jnp.float32))
        _, sel = lax.top_k(imp, n_sel)
        ks = ks_blk[bi, gi, sel].reshape(bsz, NSA_KV, Q_BLOCK, n_sel * SLC_BLOCK, HEAD_DIM)
        vs = vs_blk[bi, gi, sel].reshape(bsz, NSA_KV, Q_BLOCK, n_sel * SLC_BLOCK, HEAD_DIM)
        kpos = (sel[..., None] * SLC_BLOCK + jnp.arange(SLC_BLOCK)).reshape(bsz, NSA_KV, Q_BLOCK, n_sel * SLC_BLOCK)
        valid_s = (kpos <= t[None, None, :, None])[:, :, None]
        ss = jnp.einsum('btghd,bgtkd->bghtk', qb, ks).astype(jnp.float32) * scale
        p_s = _masked_softmax(ss, valid_s)
        o_s = jnp.einsum('bghtk,bgtkd->btghd', p_s.astype(vs.dtype), vs)
        kw = lax.dynamic_slice_in_dim(kw_pad, s0, Q_BLOCK + WINDOW, axis=1)
        vw = lax.dynamic_slice_in_dim(vw_pad, s0, Q_BLOCK + WINDOW, axis=1)
        pw = s0 - WINDOW + jnp.arange(Q_BLOCK + WINDOW)
        valid_w = (pw[None, :] <= t[:, None]) & (pw[None, :] > t[:, None] - WINDOW) & (pw[None, :] >= 0)
        sw = jnp.einsum('btghd,bkgd->bghtk', qb, kw).astype(jnp.float32) * scale
        p_w = _masked_softmax(sw, valid_w)
        o_w = jnp.einsum('bghtk,bkgd->btghd', p_w.astype(vw.dtype), vw)
        return jnp.stack([o_c, o_s, o_w], axis=-2)

    outs = lax.map(query_block, jnp.arange(s // Q_BLOCK) * Q_BLOCK)
    outs = jnp.transpose(outs, (1, 0, 2, 3, 4, 5, 6)).reshape(bsz, s, NSA_HEADS, 3, HEAD_DIM)
    gates = jax.nn.sigmoid(g_logits.reshape(bsz, s, NSA_HEADS, 3))
    return jnp.einsum('bshrd,bshr->bshd', outs, gates).reshape(bsz, s, NSA_WIDTH)


def memory_attention(qm, mem, w_kv):
    bsz, s, _ = qm.shape
    k, v = jnp.split(mem @ w_kv, 2, axis=-1)
    q = qm.reshape(bsz, s, MEM_HEADS, HEAD_DIM)
    k = k.reshape(bsz, -1, MEM_HEADS, HEAD_DIM)
    v = v.reshape(bsz, -1, MEM_HEADS, HEAD_DIM)
    sc = jnp.einsum('bshd,bmhd->bhsm', q, k).astype(jnp.float32) * HEAD_DIM ** -0.5
    p = jax.nn.softmax(sc, axis=-1).astype(v.dtype)
    return jnp.einsum('bhsm,bmhd->bshd', p, v).reshape(bsz, s, MEM_WIDTH)


def hybrid_layer(x, mem, w_in, sgu_ln_g, sgu_ln_b, sgu_w, sgu_b, conv_w, conv_b, lru_wa, lru_ba,
                 lru_wx, lru_bx, lru_lambda, cmp_pos, cmp_w1, cmp_w2, w_mem_kv, w_out, ln_g, ln_b):
    offsets = np.cumsum(IN_SIZES)[:-1].tolist()
    (u, v, z_a, x_b, z_b, q_c, z_c, kc, vc, ks, vs, kw, vw, g_c, q_d, z_d) = jnp.split(x @ w_in, offsets, axis=-1)
    y_a = spatial_gating(u, v, sgu_ln_g, sgu_ln_b, sgu_w, sgu_b) * jax.nn.silu(z_a)
    y_b = rglru_branch(x_b, conv_w, conv_b, lru_wa, lru_ba, lru_wx, lru_bx, lru_lambda) * jax.nn.silu(z_b)
    y_c = nsa_branch(q_c, kc, vc, ks, vs, kw, vw, g_c, cmp_pos, cmp_w1, cmp_w2) * jax.nn.silu(z_c)
    y_d = memory_attention(q_d, mem, w_mem_kv) * jax.nn.silu(z_d)
    out = jnp.concatenate([y_a, y_b, y_c, y_d], axis=-1) @ w_out
    return layer_norm(DEEPNORM_ALPHA * x + out, ln_g, ln_b)


def setup_inputs(seed: int = 0) -> dict:
    key = jax.random.key(seed)
    ks = jax.random.split(key, 24)
    nrm = lambda k, shape, sc: jax.random.normal(k, shape, jnp.float32) * sc
    a8 = jax.random.uniform(ks[13], (DEPTH, LRU_WIDTH), jnp.float32, 0.9, 0.999)
    sig = a8 ** (1.0 / C_LRU)
    return {
        'x': nrm(ks[0], (BATCH, SEQ, D_MODEL), 1.0),
        'mem': nrm(ks[1], (BATCH, N_MEM, D_MODEL), 1.0),
        'w_in': nrm(ks[2], (DEPTH, D_MODEL, D_IN), D_MODEL ** -0.5),
        'sgu_ln_g': 1.0 + nrm(ks[3], (DEPTH, A_WIDTH), 0.02),
        'sgu_ln_b': nrm(ks[4], (DEPTH, A_WIDTH), 0.02),
        'sgu_w': nrm(ks[5], (DEPTH, A_GROUPS, CHUNK, CHUNK), CHUNK ** -0.5),
        'sgu_b': 1.0 + nrm(ks[6], (DEPTH, A_GROUPS, CHUNK), 0.1),
        'conv_w': nrm(ks[7], (DEPTH, CONV_W, LRU_WIDTH), CONV_W ** -0.5),
        'conv_b': nrm(ks[8], (DEPTH, LRU_WIDTH), 0.02),
        'lru_wa': nrm(ks[9], (DEPTH, LRU_BLOCKS, LRU_BLOCK, LRU_BLOCK), LRU_BLOCK ** -0.5),
        'lru_ba': nrm(ks[10], (DEPTH, LRU_WIDTH), 0.02),
        'lru_wx': nrm(ks[11], (DEPTH, LRU_BLOCKS, LRU_BLOCK, LRU_BLOCK), LRU_BLOCK ** -0.5),
        'lru_bx': nrm(ks[12], (DEPTH, LRU_WIDTH), 0.02),
        'lru_lambda': jnp.log(sig) - jnp.log1p(-sig),
        'cmp_pos': nrm(ks[14], (DEPTH, 2, CMP_LEN, HEAD_DIM), 0.02),
        'cmp_w1': nrm(ks[15], (DEPTH, 2, CMP_LEN * HEAD_DIM, CMP_HIDDEN), (CMP_LEN * HEAD_DIM) ** -0.5),
        'cmp_w2': nrm(ks[16], (DEPTH, 2, CMP_HIDDEN, HEAD_DIM), CMP_HIDDEN ** -0.5),
        'w_mem_kv': nrm(ks[17], (DEPTH, D_MODEL, 2 * MEM_WIDTH), D_MODEL ** -0.5),
        'w_out': nrm(ks[18], (DEPTH, MIX_WIDTH, D_MODEL), MIX_WIDTH ** -0.5 * DEEPNORM_BETA),
        'ln_g': 1.0 + nrm(ks[19], (DEPTH, D_MODEL), 0.02),
        'ln_b': nrm(ks[20], (DEPTH, D_MODEL), 0.02),
    }


def reference(x, mem, w_in, sgu_ln_g, sgu_ln_b, sgu_w, sgu_b, conv_w, conv_b, lru_wa, lru_ba, lru_wx,
              lru_bx, lru_lambda, cmp_pos, cmp_w1, cmp_w2, w_mem_kv, w_out, ln_g, ln_b):
    for l in range(DEPTH):
        x = hybrid_layer(x, mem, w_in[l], sgu_ln_g[l], sgu_ln_b[l], sgu_w[l], sgu_b[l], conv_w[l], conv_b[l],
                         lru_wa[l], lru_ba[l], lru_wx[l], lru_bx[l], lru_lambda[l], cmp_pos[l], cmp_w1[l],
                         cmp_w2[l], w_mem_kv[l], w_out[l], ln_g[l], ln_b[l])
    return x
```

```python
import functools

import jax
import jax.numpy as jnp
import numpy as np
from jax import lax
from jax.experimental import pallas as pl
from jax.experimental.pallas import tpu as pltpu

F32 = jnp.float32
BF16 = jnp.bfloat16

CHUNK = 128
A_GROUPS = 4
A_WIDTH = 512
LRU_WIDTH = 512
LRU_BLOCKS = 8
LRU_BLOCK = 64
CONV_W = 4
C_LRU = 8.0
HEAD_DIM = 64
NSA_KV = 2
NSA_GROUP = 4
NSA_WIDTH = 512
KV_WIDTH = 128
CMP_LEN = 32
CMP_STRIDE = 16
CMP_HIDDEN = 128
SLC_BLOCK = 64
N_SELECT = 16
WINDOW = 512
N_GATES = 24
MEM_HEADS = 4
MEM_WIDTH = 256
LN_EPS = 1e-5
NEG = -1e30
FORCE_BONUS = 1e4

LANES = 128
SEL_LANES = 128
VMEM_LIMIT = 56 * 1024 * 1024

COL_A = 0
COL_B = 1536
COL_C = 2560
COL_G = 4352
COL_D = 4608
D_IN_P = 5120

TS_PROJ = 256
TQ = 128
TK = 256
TKW = 128


def _sigmoid(x):
    return 1.0 / (1.0 + jnp.exp(-x))


def _silu(x):
    return x * _sigmoid(x)


def _gelu(x):
    return jax.nn.gelu(x, approximate=True)


def _dot(a, b):
    return jnp.dot(a, b, preferred_element_type=F32)


def _dot_nt(a, b):
    return lax.dot_general(a, b, (((1,), (1,)), ((), ())), preferred_element_type=F32)


def _proj_mix_kernel(x_ref, mem_ref, w_in_ref, sln_g_ref, sln_b_ref, sw_ref, sb_ref, cw_ref, cb_ref,
                     wbd_ref, ba_ref, bx_ref, lam_ref, wmem_ref, wout_ref,
                     part_ref, q_ref, zc_ref, gl_ref, kvc_ref, ksa_ref, vs_ref, kw_ref, vw_ref,
                     conv_buf, h_carry, memk, memv):
    s = pl.program_id(1)
    ts = x_ref.shape[1]
    xb = x_ref[0].astype(BF16)

    @pl.when(s == 0)
    def _():
        conv_buf[0:8, :] = jnp.zeros((8, LRU_WIDTH), F32)
        h_carry[...] = jnp.zeros_like(h_carry)
        kv = _dot(mem_ref[0].astype(BF16), wmem_ref[...])
        memk[...] = kv[:, :MEM_WIDTH].astype(BF16)
        memv[...] = kv[:, MEM_WIDTH:].astype(BF16)

    pa = _dot(xb, w_in_ref[:, COL_A:COL_A + 3 * A_WIDTH])
    u = _gelu(pa[:, 0:A_WIDTH])
    v = _gelu(pa[:, A_WIDTH:2 * A_WIDTH])
    za = pa[:, 2 * A_WIDTH:3 * A_WIDTH]
    row = lax.broadcasted_iota(jnp.int32, (CHUNK, CHUNK), 0)
    col = lax.broadcasted_iota(jnp.int32, (CHUNK, CHUNK), 1)
    causal = col <= row
    sv_groups = []
    for g in range(A_GROUPS):
        vg = v[:, g * CHUNK:(g + 1) * CHUNK]
        mu = jnp.mean(vg, axis=-1, keepdims=True)
        var = jnp.mean(jnp.square(vg - mu), axis=-1, keepdims=True)
        vn = (vg - mu) * lax.rsqrt(var + LN_EPS)
        vn = vn * sln_g_ref[:, g * CHUNK:(g + 1) * CHUNK] + sln_b_ref[:, g * CHUNK:(g + 1) * CHUNK]
        vnb = vn.astype(BF16)
        wg = jnp.where(causal, sw_ref[g], 0.0).astype(BF16)
        bias = sb_ref[g]
        chunks = []
        for c in range(ts // CHUNK):
            chunks.append(_dot(wg, vnb[c * CHUNK:(c + 1) * CHUNK]) + bias)
        sv_groups.append(jnp.concatenate(chunks, axis=0))
    sv = jnp.concatenate(sv_groups, axis=1)
    ya = (u * sv) * _silu(za)

    pb = _dot(xb, w_in_ref[:, COL_B:COL_B + 2 * LRU_WIDTH])
    xbv = pb[:, :LRU_WIDTH]
    zb = pb[:, LRU_WIDTH:]
    conv_buf[8:8 + ts, :] = xbv
    xc = cb_ref[...] + conv_buf[5:5 + ts, :] * cw_ref[0:1, :]
    xc = xc + conv_buf[6:6 + ts, :] * cw_ref[1:2, :]
    xc = xc + conv_buf[7:7 + ts, :] * cw_ref[2:3, :]
    xc = xc + xbv * cw_ref[3:4, :]
    conv_buf[0:8, :] = xbv[ts - 8:ts, :]
    xcb = xc.astype(BF16)
    half = LRU_WIDTH // 2
    g0 = _dot(xcb[:, :half], wbd_ref[0])
    g1 = _dot(xcb[:, half:], wbd_ref[1])
    r = _sigmoid(jnp.concatenate([g0[:, :half], g1[:, :half]], axis=1) + ba_ref[...])
    ig = _sigmoid(jnp.concatenate([g0[:, half:], g1[:, half:]], axis=1) + bx_ref[...])
    nlam = -lam_ref[...]
    softplus = jnp.maximum(nlam, 0.0) + jnp.log1p(jnp.exp(-jnp.abs(nlam)))
    log_a = (-C_LRU * r) * softplus
    a_cum = jnp.exp(log_a)
    th = jnp.tanh(log_a)
    b_cum = jnp.sqrt((-2.0 * th) / (1.0 - th)) * (ig * xc)
    d = 1
    while d < ts:
        a_sh = jnp.concatenate([jnp.ones((d, LRU_WIDTH), F32), a_cum[:ts - d]], axis=0)
        b_sh = jnp.concatenate([jnp.zeros((d, LRU_WIDTH), F32), b_cum[:ts - d]], axis=0)
        b_cum = b_cum + a_cum * b_sh
        a_cum = a_cum * a_sh
        d *= 2
    h = b_cum + a_cum * h_carry[...]
    h_carry[...] = h[ts - 1:ts, :]
    yb = h * _silu(zb)

    pd = _dot(xb, w_in_ref[:, COL_D:COL_D + 2 * MEM_WIDTH])
    qd = pd[:, :MEM_WIDTH] * (HEAD_DIM ** -0.5)
    zd = pd[:, MEM_WIDTH:]
    head_of_lane = lax.broadcasted_iota(jnp.int32, (1, MEM_WIDTH), 1) // HEAD_DIM
    od = jnp.zeros((ts, MEM_WIDTH), F32)
    for hd in range(MEM_HEADS):
        hm = head_of_lane == hd
        qh = jnp.where(hm, qd, 0.0).astype(BF16)
        sc = _dot_nt(qh, memk[...])
        m = jnp.max(sc, axis=-1, keepdims=True)
        p = jnp.exp(sc - m)
        l = jnp.sum(p, axis=-1, keepdims=True)
        pv = _dot(p.astype(BF16), memv[...])
        od = od + jnp.where(hm, pv * (1.0 / l), 0.0)
    yd = od * _silu(zd)

    y_abd = jnp.concatenate([ya, yb, yd], axis=1).astype(BF16)
    part_ref[0] = _dot(y_abd, wout_ref[...])

    pc = _dot(xb, w_in_ref[:, COL_C:COL_C + 2 * NSA_WIDTH + 6 * KV_WIDTH])
    q_ref[0] = (pc[:, :NSA_WIDTH] * (HEAD_DIM ** -0.5)).astype(BF16)
    zc_ref[0] = pc[:, NSA_WIDTH:2 * NSA_WIDTH]
    gl_ref[0] = _dot(xb, w_in_ref[:, COL_G:COL_G + NSA_KV * LANES])
    base = 2 * NSA_WIDTH
    kpos = s * ts + lax.broadcasted_iota(jnp.int32, (ts, SEL_LANES), 0)
    blk = lax.broadcasted_iota(jnp.int32, (ts, SEL_LANES), 1)
    sel_rows = jnp.where((kpos // SLC_BLOCK) == blk, -NEG, 0.0).astype(BF16)
    for g in range(NSA_KV):
        lo = g * HEAD_DIM
        kvc_ref[0, 0, g] = pc[:, base + lo:base + lo + HEAD_DIM]
        kvc_ref[0, 1, g] = pc[:, base + KV_WIDTH + lo:base + KV_WIDTH + lo + HEAD_DIM]
        ksa_ref[0, g, :, 0:SEL_LANES] = sel_rows
        ksa_ref[0, g, :, SEL_LANES:SEL_LANES + HEAD_DIM] = (
            pc[:, base + 2 * KV_WIDTH + lo:base + 2 * KV_WIDTH + lo + HEAD_DIM].astype(BF16))
        vs_ref[0, g] = pc[:, base + 3 * KV_WIDTH + lo:base + 3 * KV_WIDTH + lo + HEAD_DIM].astype(BF16)
        kw_ref[0, g] = pc[:, base + 4 * KV_WIDTH + lo:base + 4 * KV_WIDTH + lo + HEAD_DIM].astype(BF16)
        vw_ref[0, g] = pc[:, base + 5 * KV_WIDTH + lo:base + 5 * KV_WIDTH + lo + HEAD_DIM].astype(BF16)


def _proj_mix(x, mem, w_in_p, sln_g, sln_b, sw, sb, cw, cb, wbd, ba, bx, lam, wmem, wout_abd):
    bsz, seq, dm = x.shape
    ts = TS_PROJ
    n_mem = mem.shape[1]
    const2 = lambda b, s: (0, 0)
    const3 = lambda b, s: (0, 0, 0)
    in_specs = [
        pl.BlockSpec((1, ts, dm), lambda b, s: (b, s, 0)),
        pl.BlockSpec((1, n_mem, dm), lambda b, s: (b, 0, 0)),
        pl.BlockSpec(w_in_p.shape, const2),
        pl.BlockSpec(sln_g.shape, const2),
        pl.BlockSpec(sln_b.shape, const2),
        pl.BlockSpec(sw.shape, const3),
        pl.BlockSpec(sb.shape, const3),
        pl.BlockSpec(cw.shape, const2),
        pl.BlockSpec(cb.shape, const2),
        pl.BlockSpec(wbd.shape, const3),
        pl.BlockSpec(ba.shape, const2),
        pl.BlockSpec(bx.shape, const2),
        pl.BlockSpec(lam.shape, const2),
        pl.BlockSpec(wmem.shape, const2),
        pl.BlockSpec(wout_abd.shape, const2),
    ]
    kv_spec = pl.BlockSpec((1, NSA_KV, ts, HEAD_DIM), lambda b, s: (b, 0, s, 0))
    out_specs = [
        pl.BlockSpec((1, ts, dm), lambda b, s: (b, s, 0)),
        pl.BlockSpec((1, ts, NSA_WIDTH), lambda b, s: (b, s, 0)),
        pl.BlockSpec((1, ts, NSA_WIDTH), lambda b, s: (b, s, 0)),
        pl.BlockSpec((1, ts, NSA_KV * LANES), lambda b, s: (b, s, 0)),
        pl.BlockSpec((1, 2, NSA_KV, ts, HEAD_DIM), lambda b, s: (b, 0, 0, s, 0)),
        pl.BlockSpec((1, NSA_KV, ts, SEL_LANES + HEAD_DIM), lambda b, s: (b, 0, s, 0)),
        kv_spec, kv_spec, kv_spec,
    ]
    out_shape = [
        jax.ShapeDtypeStruct((bsz, seq, dm), F32),
        jax.ShapeDtypeStruct((bsz, seq, NSA_WIDTH), BF16),
        jax.ShapeDtypeStruct((bsz, seq, NSA_WIDTH), F32),
        jax.ShapeDtypeStruct((bsz, seq, NSA_KV * LANES), F32),
        jax.ShapeDtypeStruct((bsz, 2, NSA_KV, seq, HEAD_DIM), F32),
        jax.ShapeDtypeStruct((bsz, NSA_KV, seq, SEL_LANES + HEAD_DIM), BF16),
        jax.ShapeDtypeStruct((bsz, NSA_KV, seq, HEAD_DIM), BF16),
        jax.ShapeDtypeStruct((bsz, NSA_KV, seq, HEAD_DIM), BF16),
        jax.ShapeDtypeStruct((bsz, NSA_KV, seq, HEAD_DIM), BF16),
    ]
    return pl.pallas_call(
        _proj_mix_kernel,
        grid=(bsz, seq // ts),
        in_specs=in_specs,
        out_specs=out_specs,
        out_shape=out_shape,
        scratch_shapes=[
            pltpu.VMEM((ts + 8, LRU_WIDTH), F32),
            pltpu.VMEM((1, LRU_WIDTH), F32),
            pltpu.VMEM((n_mem, MEM_WIDTH), BF16),
            pltpu.VMEM((n_mem, MEM_WIDTH), BF16),
        ],
        compiler_params=pltpu.CompilerParams(
            dimension_semantics=("arbitrary", "arbitrary"), vmem_limit_bytes=VMEM_LIMIT),
        name="proj_mix",
    )(x, mem, w_in_p, sln_g, sln_b, sw, sb, cw, cb, wbd, ba, bx, lam, wmem, wout_abd)


def _compress_kernel(fc_ref, pos_ref, w1_ref, w2_ref, out_ref):
    fc = fc_ref[0, 0, 0]
    half = (CMP_LEN // 2) * HEAD_DIM
    pos = pos_ref[0]
    fa = (fc + pos[:, :half]).astype(BF16)
    fb = (fc + pos[:, half:]).astype(BF16)
    ha = _dot(fa, w1_ref[0, :half, :])
    hb = _dot(fb, w1_ref[0, half:, :])
    hb_next = jnp.concatenate([hb[1:], hb[:1]], axis=0)
    hid = _gelu(ha + hb_next)
    out_ref[0, 0, 0] = _dot(hid.astype(BF16), w2_ref[0]).astype(BF16)


def _compress(fc, pos_flat, w1, w2):
    bsz, _, _, n_chunk, width = fc.shape
    return pl.pallas_call(
        _compress_kernel,
        grid=(bsz, 2, NSA_KV),
        in_specs=[
            pl.BlockSpec((1, 1, 1, n_chunk, width), lambda b, k, g: (b, k, g, 0, 0)),
            pl.BlockSpec((1, 1, CMP_LEN * HEAD_DIM), lambda b, k, g: (k, 0, 0)),
            pl.BlockSpec((1, CMP_LEN * HEAD_DIM, CMP_HIDDEN), lambda b, k, g: (k, 0, 0)),
            pl.BlockSpec((1, CMP_HIDDEN, HEAD_DIM), lambda b, k, g: (k, 0, 0)),
        ],
        out_specs=pl.BlockSpec((1, 1, 1, n_chunk, HEAD_DIM), lambda b, k, g: (b, k, g, 0, 0)),
        out_shape=jax.ShapeDtypeStruct((bsz, 2, NSA_KV, n_chunk, HEAD_DIM), BF16),
        compiler_params=pltpu.CompilerParams(
            dimension_semantics=("arbitrary", "arbitrary", "arbitrary"), vmem_limit_bytes=VMEM_LIMIT),
        name="compress",
    )(fc, pos_flat, w1, w2)


def _nsa_kernel(q_ref, zc_ref, gl_ref, kcv_ref, ksa_ref, vs_ref, kw_ref, vw_ref, ov_ref, y_ref, *, n_sel):
    qi = pl.program_id(2)
    tq = q_ref.shape[1]
    rows = NSA_GROUP * tq
    s0 = qi * tq
    q = q_ref[0]
    qs = jnp.concatenate([q[:, h * HEAD_DIM:(h + 1) * HEAD_DIM] for h in range(NSA_GROUP)], axis=0)
    t_row = s0 + (lax.broadcasted_iota(jnp.int32, (rows, 1), 0) & (tq - 1))

    kc = kcv_ref[0, 0, 0]
    vc = kcv_ref[0, 1, 0]
    n_cmp = kc.shape[0]
    sc = _dot_nt(qs, kc)
    cmp_end = lax.broadcasted_iota(jnp.int32, (1, n_cmp), 1) * CMP_STRIDE + (CMP_LEN - 1)
    valid_c = cmp_end <= t_row
    sc = jnp.where(valid_c, sc, NEG)
    m = jnp.max(sc, axis=-1, keepdims=True)
    p = jnp.where(valid_c, jnp.exp(sc - m), 0.0)
    l = jnp.sum(p, axis=-1, keepdims=True)
    pn = p * jnp.where(l > 0.0, 1.0 / l, 0.0)
    o_c = _dot(pn.astype(BF16), vc)

    ps = pn[0:tq] + pn[tq:2 * tq] + pn[2 * tq:3 * tq] + pn[3 * tq:4 * tq]
    ps_hi = ps.astype(BF16)
    ps_lo = (ps - ps_hi.astype(F32)).astype(BF16)
    imp = _dot(ps_hi, ov_ref[...]) + _dot(ps_lo, ov_ref[...])
    t_q = s0 + lax.broadcasted_iota(jnp.int32, (tq, 1), 0)
    cur = t_q // SLC_BLOCK
    blk = lax.broadcasted_iota(jnp.int32, (1, SEL_LANES), 1)
    future = blk > cur
    forced = (blk == 0) | (blk == cur) | (blk == cur - 1)
    work = jnp.where(future, -1.0, imp + jnp.where(forced, FORCE_BONUS, 0.0))
    selm1 = jnp.full((tq, SEL_LANES), -1.0, F32)
    blk_f = blk.astype(F32)
    for _ in range(n_sel):
        mx = jnp.max(work, axis=-1, keepdims=True)
        idx = jnp.min(jnp.where(work == mx, blk_f, float(SEL_LANES)), axis=-1, keepdims=True)
        hit = blk_f == idx
        selm1 = jnp.where(hit, 0.0, selm1)
        work = jnp.where(hit, -jnp.inf, work)
    sel_rows = jnp.concatenate([selm1.astype(BF16)] * NSA_GROUP, axis=0)
    q_aug = jnp.concatenate([sel_rows, qs], axis=1)

    def slc_tile(kt, carry, diagonal):
        m_i, l_i, acc = carry
        start = pl.multiple_of(kt * TK, TK)
        k = ksa_ref[0, 0, pl.ds(start, TK), :]
        v = vs_ref[0, 0, pl.ds(start, TK), :]
        s = _dot_nt(q_aug, k)
        if diagonal:
            kpos = start + lax.broadcasted_iota(jnp.int32, (1, TK), 1)
            s = jnp.where(kpos <= t_row, s, NEG)
        m_new = jnp.maximum(m_i, jnp.max(s, axis=-1, keepdims=True))
        alpha = jnp.exp(m_i - m_new)
        p_t = jnp.exp(s - m_new)
        l_new = alpha * l_i + jnp.sum(p_t, axis=-1, keepdims=True)
        acc_new = alpha * acc + _dot(p_t.astype(BF16), v)
        return m_new, l_new, acc_new

    init = (jnp.full((rows, 1), NEG, F32), jnp.zeros((rows, 1), F32), jnp.zeros((rows, HEAD_DIM), F32))
    n_full = s0 // TK
    carry = lax.fori_loop(0, n_full, lambda kt, c: slc_tile(kt, c, False), init)
    _, l_s, acc_s = slc_tile(n_full, carry, True)
    o_s = acc_s * (1.0 / l_s)

    def win_tile(kt, carry):
        m_i, l_i, acc = carry
        start = pl.multiple_of(kt * TKW, TKW)
        k = kw_ref[0, 0, pl.ds(start, TKW), :]
        v = vw_ref[0, 0, pl.ds(start, TKW), :]
        s = _dot_nt(qs, k)
        kpos = start + lax.broadcasted_iota(jnp.int32, (1, TKW), 1)
        valid = (kpos <= t_row) & (kpos > t_row - WINDOW)
        s = jnp.where(valid, s, NEG)
        m_new = jnp.maximum(m_i, jnp.max(s, axis=-1, keepdims=True))
        alpha = jnp.exp(m_i - m_new)
        p_t = jnp.where(valid, jnp.exp(s - m_new), 0.0)
        l_new = alpha * l_i + jnp.sum(p_t, axis=-1, keepdims=True)
        acc_new = alpha * acc + _dot(p_t.astype(BF16), v)
        return m_new, l_new, acc_new

    last = s0 // TKW
    first = jnp.maximum(last - WINDOW // TKW, 0)
    _, l_w, acc_w = lax.fori_loop(first, last + 1, win_tile, init)
    o_w = acc_w * (1.0 / l_w)

    gates = _sigmoid(gl_ref[0])
    heads = []
    for h in range(NSA_GROUP):
        y_h = o_c[h * tq:(h + 1) * tq] * gates[:, 3 * h:3 * h + 1]
        y_h = y_h + o_s[h * tq:(h + 1) * tq] * gates[:, 3 * h + 1:3 * h + 2]
        y_h = y_h + o_w[h * tq:(h + 1) * tq] * gates[:, 3 * h + 2:3 * h + 3]
        heads.append(y_h)
    y = jnp.concatenate(heads, axis=1)
    y_ref[0] = (y * _silu(zc_ref[0])).astype(BF16)


def _nsa(q, zc, gl, kcv, ksa, vs, kw, vw, overlap):
    bsz, seq, _ = q.shape
    n_cmp = kcv.shape[3]
    gw = NSA_GROUP * HEAD_DIM
    n_sel = min(N_SELECT, seq // SLC_BLOCK)
    per_bg = lambda b, g, i: (b, g, 0, 0)
    return pl.pallas_call(
        functools.partial(_nsa_kernel, n_sel=n_sel),
        grid=(bsz, NSA_KV, seq // TQ),
        in_specs=[
            pl.BlockSpec((1, TQ, gw), lambda b, g, i: (b, i, g)),
            pl.BlockSpec((1, TQ, gw), lambda b, g, i: (b, i, g)),
            pl.BlockSpec((1, TQ, LANES), lambda b, g, i: (b, i, g)),
            pl.BlockSpec((1, 2, 1, n_cmp, HEAD_DIM), lambda b, g, i: (b, 0, g, 0, 0)),
            pl.BlockSpec((1, 1, seq, SEL_LANES + HEAD_DIM), per_bg),
            pl.BlockSpec((1, 1, seq, HEAD_DIM), per_bg),
            pl.BlockSpec((1, 1, seq, HEAD_DIM), per_bg),
            pl.BlockSpec((1, 1, seq, HEAD_DIM), per_bg),
            pl.BlockSpec(overlap.shape, lambda b, g, i: (0, 0)),
        ],
        out_specs=pl.BlockSpec((1, TQ, gw), lambda b, g, i: (b, i, g)),
        out_shape=jax.ShapeDtypeStruct((bsz, seq, NSA_WIDTH), BF16),
        compiler_params=pltpu.CompilerParams(
            dimension_semantics=("arbitrary", "arbitrary", "arbitrary"), vmem_limit_bytes=VMEM_LIMIT),
        name="nsa",
    )(q, zc, gl, kcv, ksa, vs, kw, vw, overlap)


def _final_kernel(x_ref, part_ref, yc_ref, wout_ref, g_ref, b_ref, o_ref, *, alpha):
    out = part_ref[0] + _dot(yc_ref[0], wout_ref[...])
    y = alpha * x_ref[0] + out
    mu = jnp.mean(y, axis=-1, keepdims=True)
    var = jnp.mean(jnp.square(y - mu), axis=-1, keepdims=True)
    o_ref[0] = ((y - mu) * lax.rsqrt(var + LN_EPS)) * g_ref[...] + b_ref[...]


def _final(x, part, yc, wout_c, ln_g, ln_b, alpha):
    bsz, seq, dm = x.shape
    ts = TS_PROJ
    tok = lambda b, s: (b, s, 0)
    return pl.pallas_call(
        functools.partial(_final_kernel, alpha=alpha),
        grid=(bsz, seq // ts),
        in_specs=[
            pl.BlockSpec((1, ts, dm), tok),
            pl.BlockSpec((1, ts, dm), tok),
            pl.BlockSpec((1, ts, NSA_WIDTH), tok),
            pl.BlockSpec(wout_c.shape, lambda b, s: (0, 0)),
            pl.BlockSpec(ln_g.shape, lambda b, s: (0, 0)),
            pl.BlockSpec(ln_b.shape, lambda b, s: (0, 0)),
        ],
        out_specs=pl.BlockSpec((1, ts, dm), tok),
        out_shape=jax.ShapeDtypeStruct((bsz, seq, dm), F32),
        compiler_params=pltpu.CompilerParams(
            dimension_semantics=("arbitrary", "arbitrary"), vmem_limit_bytes=VMEM_LIMIT),
        name="final",
    )(x, part, yc, wout_c, ln_g, ln_b)


def _pack_w_in(w_in):
    depth, dm, _ = w_in.shape
    gate0 = COL_C + 2 * NSA_WIDTH + 6 * KV_WIDTH
    per_group = N_GATES // NSA_KV
    gates = w_in[:, :, gate0:gate0 + N_GATES]
    gate_blocks = []
    for g in range(NSA_KV):
        blk = gates[:, :, g * per_group:(g + 1) * per_group]
        gate_blocks.append(jnp.pad(blk, ((0, 0), (0, 0), (0, LANES - per_group))))
    d_cols = w_in[:, :, gate0 + N_GATES:]
    return jnp.concatenate([w_in[:, :, :gate0]] + gate_blocks + [d_cols], axis=-1).astype(BF16)


def _block_diag_gates(wa, wx):
    depth = wa.shape[0]
    per_half = LRU_BLOCKS // 2
    half = per_half * LRU_BLOCK
    halves = []
    for hf in range(2):
        mats = []
        for w in (wa, wx):
            m = jnp.zeros((depth, half, half), F32)
            for j in range(per_half):
                m = m.at[:, j * LRU_BLOCK:(j + 1) * LRU_BLOCK, j * LRU_BLOCK:(j + 1) * LRU_BLOCK].set(
                    w[:, hf * per_half + j])
            mats.append(m)
        halves.append(jnp.concatenate(mats, axis=-1))
    return jnp.stack(halves, axis=1).astype(BF16)


def _overlap_matrix(n_cmp_pad, n_cmp, n_slc):
    ci = np.arange(n_cmp_pad)[:, None] * CMP_STRIDE
    sj = np.arange(SEL_LANES)[None, :] * SLC_BLOCK
    ov = (ci < sj + SLC_BLOCK) & (ci + CMP_LEN > sj)
    ov &= (np.arange(n_cmp_pad)[:, None] < n_cmp) & (np.arange(SEL_LANES)[None, :] < n_slc)
    return jnp.asarray(ov.astype(np.float32), dtype=BF16)


def kernel(x, mem, w_in, sgu_ln_g, sgu_ln_b, sgu_w, sgu_b, conv_w, conv_b, lru_wa, lru_ba, lru_wx, lru_bx,
           lru_lambda, cmp_pos, cmp_w1, cmp_w2, w_mem_kv, w_out, ln_g, ln_b):
    depth = w_in.shape[0]
    bsz, seq, dm = x.shape
    assert seq % TK == 0 and seq // SLC_BLOCK <= SEL_LANES
    alpha = (2 * depth) ** 0.25
    n_chunk = seq // CMP_STRIDE
    n_cmp = (seq - CMP_LEN) // CMP_STRIDE + 1

    w_in_p = _pack_w_in(w_in)
    wbd = _block_diag_gates(lru_wa, lru_wx)
    wmem = w_mem_kv.astype(BF16)
    c0 = A_WIDTH + LRU_WIDTH
    wout_abd = jnp.concatenate([w_out[:, :c0], w_out[:, c0 + NSA_WIDTH:]], axis=1).astype(BF16)
    wout_c = w_out[:, c0:c0 + NSA_WIDTH].astype(BF16)
    w1 = cmp_w1.astype(BF16)
    w2 = cmp_w2.astype(BF16)
    pos_flat = cmp_pos.reshape(depth, 2, 1, CMP_LEN * HEAD_DIM)
    overlap = _overlap_matrix(n_chunk, n_cmp, seq // SLC_BLOCK)
    row = lambda a: a[:, None, :]

    for l in range(depth):
        part, q, zc, gl, kvc_raw, ksa, vs, kw, vw = _proj_mix(
            x, mem, w_in_p[l], row(sgu_ln_g)[l], row(sgu_ln_b)[l], sgu_w[l], sgu_b[l][:, :, None],
            conv_w[l], row(conv_b)[l], wbd[l], row(lru_ba)[l], row(lru_bx)[l], row(lru_lambda)[l],
            wmem[l], wout_abd[l])
        fc = kvc_raw.reshape(bsz, 2, NSA_KV, n_chunk, CMP_STRIDE * HEAD_DIM)
        kcv = _compress(fc, pos_flat[l], w1[l], w2[l])
        yc = _nsa(q, zc, gl, kcv, ksa, vs, kw, vw, overlap)
        x = _final(x, part, yc, wout_c[l], row(ln_g)[l], row(ln_b)[l], alpha)
    return x
```

```python
import functools

import jax
import jax.numpy as jnp
import numpy as np
from jax import lax
from jax.experimental import pallas as pl
from jax.experimental.pallas import tpu as pltpu

F32 = jnp.float32
BF16 = jnp.bfloat16

CHUNK = 128
A_GROUPS = 4
A_WIDTH = 512
LRU_WIDTH = 512
LRU_BLOCKS = 8
LRU_BLOCK = 64
CONV_W = 4
C_LRU = 8.0
HEAD_DIM = 64
NSA_KV = 2
NSA_GROUP = 4
NSA_WIDTH = 512
KV_WIDTH = 128
CMP_LEN = 32
CMP_STRIDE = 16
CMP_HIDDEN = 128
SLC_BLOCK = 64
N_SELECT = 16
WINDOW = 512
N_GATES = 24
MEM_HEADS = 4
MEM_WIDTH = 256
LN_EPS = 1e-5
NEG = -1e30
FORCE_BONUS = 1e4

LANES = 128
SEL_LANES = 128
VMEM_LIMIT = 56 * 1024 * 1024

COL_A = 0
COL_B = 1536
COL_C = 2560
COL_G = 4352
COL_D = 4608
D_IN_P = 5120

TS_PROJ = 256
TQ = 512
TK = 256

Q_SCALE = HEAD_DIM ** -0.5 * float(np.log2(np.e))


def _sigmoid(x):
    return 1.0 / (1.0 + jnp.exp(-x))


def _silu(x):
    return x * _sigmoid(x)


def _gelu(x):
    return jax.nn.gelu(x, approximate=True)


def _dot(a, b):
    return jnp.dot(a, b, preferred_element_type=F32)


def _dot_nt(a, b):
    return lax.dot_general(a, b, (((1,), (1,)), ((), ())), preferred_element_type=F32)


def _proj_mix_kernel(x_ref, mem_ref, w_in_ref, sln_g_ref, sln_b_ref, sw_ref, sb_ref, cw_ref, cb_ref,
                     wbd_ref, ba_ref, bx_ref, lam_ref, wmem_ref, wout_ref,
                     part_ref, q_ref, zc_ref, gl_ref, kvc_ref, ksa_ref, vs_ref, kw_ref, vw_ref,
                     conv_buf, h_carry, memk, memv):
    s = pl.program_id(1)
    ts = x_ref.shape[1]
    xb = x_ref[0].astype(BF16)

    @pl.when(s == 0)
    def _():
        conv_buf[0:8, :] = jnp.zeros((8, LRU_WIDTH), F32)
        h_carry[...] = jnp.zeros_like(h_carry)
        kv = _dot(mem_ref[0].astype(BF16), wmem_ref[...])
        memk[...] = kv[:, :MEM_WIDTH].astype(BF16)
        memv[...] = kv[:, MEM_WIDTH:].astype(BF16)

    pa = _dot(xb, w_in_ref[:, COL_A:COL_A + 3 * A_WIDTH])
    u = _gelu(pa[:, 0:A_WIDTH])
    v = _gelu(pa[:, A_WIDTH:2 * A_WIDTH])
    za = pa[:, 2 * A_WIDTH:3 * A_WIDTH]
    row = lax.broadcasted_iota(jnp.int32, (CHUNK, CHUNK), 0)
    col = lax.broadcasted_iota(jnp.int32, (CHUNK, CHUNK), 1)
    causal = col <= row
    sv_groups = []
    for g in range(A_GROUPS):
        vg = v[:, g * CHUNK:(g + 1) * CHUNK]
        mu = jnp.mean(vg, axis=-1, keepdims=True)
        var = jnp.mean(jnp.square(vg - mu), axis=-1, keepdims=True)
        vn = (vg - mu) * lax.rsqrt(var + LN_EPS)
        vn = vn * sln_g_ref[:, g * CHUNK:(g + 1) * CHUNK] + sln_b_ref[:, g * CHUNK:(g + 1) * CHUNK]
        vnb = vn.astype(BF16)
        wg = jnp.where(causal, sw_ref[g], 0.0).astype(BF16)
        bias = sb_ref[g]
        chunks = []
        for c in range(ts // CHUNK):
            chunks.append(_dot(wg, vnb[c * CHUNK:(c + 1) * CHUNK]) + bias)
        sv_groups.append(jnp.concatenate(chunks, axis=0))
    sv = jnp.concatenate(sv_groups, axis=1)
    ya = (u * sv) * _silu(za)

    pb = _dot(xb, w_in_ref[:, COL_B:COL_B + 2 * LRU_WIDTH])
    xbv = pb[:, :LRU_WIDTH]
    zb = pb[:, LRU_WIDTH:]
    conv_buf[8:8 + ts, :] = xbv
    xc = cb_ref[...] + conv_buf[5:5 + ts, :] * cw_ref[0:1, :]
    xc = xc + conv_buf[6:6 + ts, :] * cw_ref[1:2, :]
    xc = xc + conv_buf[7:7 + ts, :] * cw_ref[2:3, :]
    xc = xc + xbv * cw_ref[3:4, :]
    conv_buf[0:8, :] = xbv[ts - 8:ts, :]
    xcb = xc.astype(BF16)
    half = LRU_WIDTH // 2
    g0 = _dot(xcb[:, :half], wbd_ref[0])
    g1 = _dot(xcb[:, half:], wbd_ref[1])
    r = _sigmoid(jnp.concatenate([g0[:, :half], g1[:, :half]], axis=1) + ba_ref[...])
    ig = _sigmoid(jnp.concatenate([g0[:, half:], g1[:, half:]], axis=1) + bx_ref[...])
    nlam = -lam_ref[...]
    softplus = jnp.maximum(nlam, 0.0) + jnp.log1p(jnp.exp(-jnp.abs(nlam)))
    log_a = (-C_LRU * r) * softplus
    a_cum = jnp.exp(log_a)
    th = jnp.tanh(log_a)
    b_cum = jnp.sqrt((-2.0 * th) / (1.0 - th)) * (ig * xc)
    d = 1
    while d < ts:
        a_sh = jnp.concatenate([jnp.ones((d, LRU_WIDTH), F32), a_cum[:ts - d]], axis=0)
        b_sh = jnp.concatenate([jnp.zeros((d, LRU_WIDTH), F32), b_cum[:ts - d]], axis=0)
        b_cum = b_cum + a_cum * b_sh
        a_cum = a_cum * a_sh
        d *= 2
    h = b_cum + a_cum * h_carry[...]
    h_carry[...] = h[ts - 1:ts, :]
    yb = h * _silu(zb)

    pd = _dot(xb, w_in_ref[:, COL_D:COL_D + 2 * MEM_WIDTH])
    qd = pd[:, :MEM_WIDTH] * (HEAD_DIM ** -0.5)
    zd = pd[:, MEM_WIDTH:]
    head_of_lane = lax.broadcasted_iota(jnp.int32, (1, MEM_WIDTH), 1) // HEAD_DIM
    od = jnp.zeros((ts, MEM_WIDTH), F32)
    for hd in range(MEM_HEADS):
        hm = head_of_lane == hd
        qh = jnp.where(hm, qd, 0.0).astype(BF16)
        sc = _dot_nt(qh, memk[...])
        m = jnp.max(sc, axis=-1, keepdims=True)
        p = jnp.exp(sc - m)
        l = jnp.sum(p, axis=-1, keepdims=True)
        pv = _dot(p.astype(BF16), memv[...])
        od = od + jnp.where(hm, pv * (1.0 / l), 0.0)
    yd = od * _silu(zd)

    y_abd = jnp.concatenate([ya, yb, yd], axis=1).astype(BF16)
    part_ref[0] = _dot(y_abd, wout_ref[...])

    pc = _dot(xb, w_in_ref[:, COL_C:COL_C + 2 * NSA_WIDTH + 6 * KV_WIDTH])
    q_ref[0] = (pc[:, :NSA_WIDTH] * Q_SCALE).astype(BF16)
    zc_ref[0] = pc[:, NSA_WIDTH:2 * NSA_WIDTH]
    gl_ref[0] = _dot(xb, w_in_ref[:, COL_G:COL_G + NSA_KV * LANES])
    base = 2 * NSA_WIDTH
    kpos = s * ts + lax.broadcasted_iota(jnp.int32, (ts, SEL_LANES), 0)
    blk = lax.broadcasted_iota(jnp.int32, (ts, SEL_LANES), 1)
    sel_rows = jnp.where((kpos // SLC_BLOCK) == blk, -NEG, 0.0).astype(BF16)
    ones_col = jnp.where(lax.broadcasted_iota(jnp.int32, (ts, HEAD_DIM), 1) == 0, 1.0, 0.0).astype(BF16)
    for g in range(NSA_KV):
        lo = g * HEAD_DIM
        kvc_ref[0, 0, g] = pc[:, base + lo:base + lo + HEAD_DIM]
        kvc_ref[0, 1, g] = pc[:, base + KV_WIDTH + lo:base + KV_WIDTH + lo + HEAD_DIM]
        ksa_ref[0, g, :, 0:SEL_LANES] = sel_rows
        ksa_ref[0, g, :, SEL_LANES:SEL_LANES + HEAD_DIM] = (
            pc[:, base + 2 * KV_WIDTH + lo:base + 2 * KV_WIDTH + lo + HEAD_DIM].astype(BF16))
        vs_ref[0, g, :, 0:HEAD_DIM] = (
            pc[:, base + 3 * KV_WIDTH + lo:base + 3 * KV_WIDTH + lo + HEAD_DIM].astype(BF16))
        vs_ref[0, g, :, HEAD_DIM:2 * HEAD_DIM] = ones_col
        kw_ref[0, g] = pc[:, base + 4 * KV_WIDTH + lo:base + 4 * KV_WIDTH + lo + HEAD_DIM].astype(BF16)
        vw_ref[0, g, :, 0:HEAD_DIM] = (
            pc[:, base + 5 * KV_WIDTH + lo:base + 5 * KV_WIDTH + lo + HEAD_DIM].astype(BF16))
        vw_ref[0, g, :, HEAD_DIM:2 * HEAD_DIM] = ones_col


def _proj_mix(x, mem, w_in_p, sln_g, sln_b, sw, sb, cw, cb, wbd, ba, bx, lam, wmem, wout_abd):
    bsz, seq, dm = x.shape
    ts = TS_PROJ
    n_mem = mem.shape[1]
    const2 = lambda b, s: (0, 0)
    const3 = lambda b, s: (0, 0, 0)
    in_specs = [
        pl.BlockSpec((1, ts, dm), lambda b, s: (b, s, 0)),
        pl.BlockSpec((1, n_mem, dm), lambda b, s: (b, 0, 0)),
        pl.BlockSpec(w_in_p.shape, const2),
        pl.BlockSpec(sln_g.shape, const2),
        pl.BlockSpec(sln_b.shape, const2),
        pl.BlockSpec(sw.shape, const3),
        pl.BlockSpec(sb.shape, const3),
        pl.BlockSpec(cw.shape, const2),
        pl.BlockSpec(cb.shape, const2),
        pl.BlockSpec(wbd.shape, const3),
        pl.BlockSpec(ba.shape, const2),
        pl.BlockSpec(bx.shape, const2),
        pl.BlockSpec(lam.shape, const2),
        pl.BlockSpec(wmem.shape, const2),
        pl.BlockSpec(wout_abd.shape, const2),
    ]
    kv_spec = pl.BlockSpec((1, NSA_KV, ts, HEAD_DIM), lambda b, s: (b, 0, s, 0))
    va_spec = pl.BlockSpec((1, NSA_KV, ts, 2 * HEAD_DIM), lambda b, s: (b, 0, s, 0))
    out_specs = [
        pl.BlockSpec((1, ts, dm), lambda b, s: (b, s, 0)),
        pl.BlockSpec((1, ts, NSA_WIDTH), lambda b, s: (b, s, 0)),
        pl.BlockSpec((1, ts, NSA_WIDTH), lambda b, s: (b, s, 0)),
        pl.BlockSpec((1, ts, NSA_KV * LANES), lambda b, s: (b, s, 0)),
        pl.BlockSpec((1, 2, NSA_KV, ts, HEAD_DIM), lambda b, s: (b, 0, 0, s, 0)),
        pl.BlockSpec((1, NSA_KV, ts, SEL_LANES + HEAD_DIM), lambda b, s: (b, 0, s, 0)),
        va_spec, kv_spec, va_spec,
    ]
    out_shape = [
        jax.ShapeDtypeStruct((bsz, seq, dm), F32),
        jax.ShapeDtypeStruct((bsz, seq, NSA_WIDTH), BF16),
        jax.ShapeDtypeStruct((bsz, seq, NSA_WIDTH), F32),
        jax.ShapeDtypeStruct((bsz, seq, NSA_KV * LANES), F32),
        jax.ShapeDtypeStruct((bsz, 2, NSA_KV, seq, HEAD_DIM), F32),
        jax.ShapeDtypeStruct((bsz, NSA_KV, seq, SEL_LANES + HEAD_DIM), BF16),
        jax.ShapeDtypeStruct((bsz, NSA_KV, seq, 2 * HEAD_DIM), BF16),
        jax.ShapeDtypeStruct((bsz, NSA_KV, seq, HEAD_DIM), BF16),
        jax.ShapeDtypeStruct((bsz, NSA_KV, seq, 2 * HEAD_DIM), BF16),
    ]
    return pl.pallas_call(
        _proj_mix_kernel,
        grid=(bsz, seq // ts),
        in_specs=in_specs,
        out_specs=out_specs,
        out_shape=out_shape,
        scratch_shapes=[
            pltpu.VMEM((ts + 8, LRU_WIDTH), F32),
            pltpu.VMEM((1, LRU_WIDTH), F32),
            pltpu.VMEM((n_mem, MEM_WIDTH), BF16),
            pltpu.VMEM((n_mem, MEM_WIDTH), BF16),
        ],
        compiler_params=pltpu.CompilerParams(
            dimension_semantics=("arbitrary", "arbitrary"), vmem_limit_bytes=VMEM_LIMIT),
        name="proj_mix",
    )(x, mem, w_in_p, sln_g, sln_b, sw, sb, cw, cb, wbd, ba, bx, lam, wmem, wout_abd)


def _compress_kernel(fc_ref, pos_ref, w1_ref, w2_ref, out_ref):
    fc = fc_ref[0, 0, 0]
    half = (CMP_LEN // 2) * HEAD_DIM
    pos = pos_ref[0]
    fa = (fc + pos[:, :half]).astype(BF16)
    fb = (fc + pos[:, half:]).astype(BF16)
    ha = _dot(fa, w1_ref[0, :half, :])
    hb = _dot(fb, w1_ref[0, half:, :])
    hb_next = jnp.concatenate([hb[1:], hb[:1]], axis=0)
    hid = _gelu(ha + hb_next)
    out_ref[0, 0, 0] = _dot(hid.astype(BF16), w2_ref[0]).astype(BF16)


def _compress(fc, pos_flat, w1, w2):
    bsz, _, _, n_chunk, width = fc.shape
    return pl.pallas_call(
        _compress_kernel,
        grid=(bsz, 2, NSA_KV),
        in_specs=[
            pl.BlockSpec((1, 1, 1, n_chunk, width), lambda b, k, g: (b, k, g, 0, 0)),
            pl.BlockSpec((1, 1, CMP_LEN * HEAD_DIM), lambda b, k, g: (k, 0, 0)),
            pl.BlockSpec((1, CMP_LEN * HEAD_DIM, CMP_HIDDEN), lambda b, k, g: (k, 0, 0)),
            pl.BlockSpec((1, CMP_HIDDEN, HEAD_DIM), lambda b, k, g: (k, 0, 0)),
        ],
        out_specs=pl.BlockSpec((1, 1, 1, n_chunk, HEAD_DIM), lambda b, k, g: (b, k, g, 0, 0)),
        out_shape=jax.ShapeDtypeStruct((bsz, 2, NSA_KV, n_chunk, HEAD_DIM), BF16),
        compiler_params=pltpu.CompilerParams(
            dimension_semantics=("arbitrary", "arbitrary", "arbitrary"), vmem_limit_bytes=VMEM_LIMIT),
        name="compress",
    )(fc, pos_flat, w1, w2)


def _nsa_kernel(q_ref, zc_ref, gl_ref, kcv_ref, ksa_ref, vs_ref, kw_ref, vw_ref, ov_ref, y_ref, s_buf, *, n_sel):
    qi = pl.program_id(2)
    tq = q_ref.shape[1]
    rows = NSA_GROUP * tq
    s0 = qi * tq
    q = q_ref[0]
    q_heads = [q[:, h * HEAD_DIM:(h + 1) * HEAD_DIM] for h in range(NSA_GROUP)]
    qs = jnp.concatenate(q_heads, axis=0)
    t_q = s0 + lax.broadcasted_iota(jnp.int32, (tq, 1), 0)
    t_row = s0 + (lax.broadcasted_iota(jnp.int32, (rows, 1), 0) & (tq - 1))

    kc = kcv_ref[0, 0, 0]
    vc = kcv_ref[0, 1, 0]
    n_cmp = kc.shape[0]
    cmp_end = lax.broadcasted_iota(jnp.int32, (1, n_cmp), 1) * CMP_STRIDE + (CMP_LEN - 1)
    valid_c = cmp_end <= t_q
    o_c = []
    ps = None
    for h in range(NSA_GROUP):
        sc = jnp.where(valid_c, _dot_nt(q_heads[h], kc), NEG)
        m = jnp.max(sc, axis=-1, keepdims=True)
        p = jnp.where(valid_c, jnp.exp2(sc - m), 0.0)
        l = jnp.sum(p, axis=-1, keepdims=True)
        pn = p * jnp.where(l > 0.0, 1.0 / l, 0.0)
        o_c.append(_dot(pn.astype(BF16), vc))
        ps = pn if ps is None else ps + pn

    ps_hi = ps.astype(BF16)
    ps_lo = (ps - ps_hi.astype(F32)).astype(BF16)
    imp = _dot(ps_hi, ov_ref[...]) + _dot(ps_lo, ov_ref[...])
    cur = t_q // SLC_BLOCK
    blk = lax.broadcasted_iota(jnp.int32, (1, SEL_LANES), 1)
    future = blk > cur
    forced = (blk == 0) | (blk == cur) | (blk == cur - 1)
    work = jnp.where(future, -1.0, imp + jnp.where(forced, FORCE_BONUS, 0.0))
    selm1 = jnp.full((tq, SEL_LANES), -1.0, F32)
    blk_f = blk.astype(F32)
    for _ in range(n_sel):
        mx = jnp.max(work, axis=-1, keepdims=True)
        idx = jnp.min(jnp.where(work == mx, blk_f, float(SEL_LANES)), axis=-1, keepdims=True)
        hit = blk_f == idx
        selm1 = jnp.where(hit, 0.0, selm1)
        work = jnp.where(hit, -jnp.inf, work)
    sel_rows = jnp.concatenate([selm1.astype(BF16)] * NSA_GROUP, axis=0)
    q_aug = jnp.concatenate([sel_rows, qs], axis=1)

    def online_step(s, v, carry, valid):
        m_i, acc = carry
        if valid is not None:
            s = jnp.where(valid, s, NEG)
        m_new = jnp.maximum(m_i, jnp.max(s, axis=-1, keepdims=True))
        p_t = jnp.exp2(s - m_new)
        if valid is not None:
            p_t = jnp.where(valid, p_t, 0.0)
        return m_new, jnp.exp2(m_i - m_new) * acc + _dot(p_t.astype(BF16), v)

    def slc_scores(kt):
        start = pl.multiple_of(kt * TK, TK)
        return _dot_nt(q_aug, ksa_ref[0, 0, pl.ds(start, TK), :])

    def slc_values(kt):
        return vs_ref[0, 0, pl.ds(pl.multiple_of(kt * TK, TK), TK), :]

    per_q = tq // TK
    s_buf[0] = slc_scores(0)

    def slc_pair(j, carry):
        for half in range(per_q):
            kt = per_q * j + half
            s = s_buf[half % 2]
            s_buf[(half + 1) % 2] = slc_scores(kt + 1)
            carry = online_step(s, slc_values(kt), carry, None)
        return carry

    init = (jnp.full((rows, 1), NEG, F32), jnp.zeros((rows, 2 * HEAD_DIM), F32))
    carry = lax.fori_loop(0, qi, slc_pair, init)
    n_full = per_q * qi
    for half in range(per_q):
        kt = n_full + half
        s = s_buf[half % 2]
        if half + 1 < per_q:
            s_buf[(half + 1) % 2] = slc_scores(kt + 1)
        valid = (kt * TK + lax.broadcasted_iota(jnp.int32, (1, TK), 1)) <= t_row
        carry = online_step(s, slc_values(kt), carry, valid)
    acc_s = carry[1]
    o_s = acc_s[:, :HEAD_DIM] * (1.0 / acc_s[:, HEAD_DIM:HEAD_DIM + 1])

    carry = init
    for j in range((WINDOW + tq) // TK):
        kt = (s0 - WINDOW) // TK + j
        start = pl.multiple_of(jnp.maximum(kt, 0) * TK, TK)
        base = jnp.where(kt >= 0, kt * TK, -(1 << 24))
        dist = t_row - (base + lax.broadcasted_iota(jnp.int32, (1, TK), 1))
        valid = dist.astype(jnp.uint32) < jnp.uint32(WINDOW)
        s = _dot_nt(qs, kw_ref[0, 0, pl.ds(start, TK), :])
        carry = online_step(s, vw_ref[0, 0, pl.ds(start, TK), :], carry, valid)
    acc_w = carry[1]
    o_w = acc_w[:, :HEAD_DIM] * (1.0 / acc_w[:, HEAD_DIM:HEAD_DIM + 1])

    gates = _sigmoid(gl_ref[0])
    heads = []
    for h in range(NSA_GROUP):
        y_h = o_c[h] * gates[:, 3 * h:3 * h + 1]
        y_h = y_h + o_s[h * tq:(h + 1) * tq] * gates[:, 3 * h + 1:3 * h + 2]
        y_h = y_h + o_w[h * tq:(h + 1) * tq] * gates[:, 3 * h + 2:3 * h + 3]
        heads.append(y_h)
    y = jnp.concatenate(heads, axis=1)
    y_ref[0] = (y * _silu(zc_ref[0])).astype(BF16)


def _nsa(q, zc, gl, kcv, ksa, vs, kw, vw, overlap):
    bsz, seq, _ = q.shape
    n_cmp = kcv.shape[3]
    gw = NSA_GROUP * HEAD_DIM
    n_sel = min(N_SELECT, seq // SLC_BLOCK)
    per_bg = lambda b, g, i: (b, g, 0, 0)
    return pl.pallas_call(
        functools.partial(_nsa_kernel, n_sel=n_sel),
        grid=(bsz, NSA_KV, seq // TQ),
        in_specs=[
            pl.BlockSpec((1, TQ, gw), lambda b, g, i: (b, i, g)),
            pl.BlockSpec((1, TQ, gw), lambda b, g, i: (b, i, g)),
            pl.BlockSpec((1, TQ, LANES), lambda b, g, i: (b, i, g)),
            pl.BlockSpec((1, 2, 1, n_cmp, HEAD_DIM), lambda b, g, i: (b, 0, g, 0, 0)),
            pl.BlockSpec((1, 1, seq, SEL_LANES + HEAD_DIM), per_bg),
            pl.BlockSpec((1, 1, seq, 2 * HEAD_DIM), per_bg),
            pl.BlockSpec((1, 1, seq, HEAD_DIM), per_bg),
            pl.BlockSpec((1, 1, seq, 2 * HEAD_DIM), per_bg),
            pl.BlockSpec(overlap.shape, lambda b, g, i: (0, 0)),
        ],
        out_specs=pl.BlockSpec((1, TQ, gw), lambda b, g, i: (b, i, g)),
        out_shape=jax.ShapeDtypeStruct((bsz, seq, NSA_WIDTH), BF16),
        scratch_shapes=[pltpu.VMEM((2, NSA_GROUP * TQ, TK), F32)],
        compiler_params=pltpu.CompilerParams(
            dimension_semantics=("arbitrary", "arbitrary", "arbitrary"), vmem_limit_bytes=VMEM_LIMIT),
        name="nsa",
    )(q, zc, gl, kcv, ksa, vs, kw, vw, overlap)


def _final_kernel(x_ref, part_ref, yc_ref, wout_ref, g_ref, b_ref, o_ref, *, alpha):
    out = part_ref[0] + _dot(yc_ref[0], wout_ref[...])
    y = alpha * x_ref[0] + out
    mu = jnp.mean(y, axis=-1, keepdims=True)
    var = jnp.mean(jnp.square(y - mu), axis=-1, keepdims=True)
    o_ref[0] = ((y - mu) * lax.rsqrt(var + LN_EPS)) * g_ref[...] + b_ref[...]


def _final(x, part, yc, wout_c, ln_g, ln_b, alpha):
    bsz, seq, dm = x.shape
    ts = TS_PROJ
    tok = lambda b, s: (b, s, 0)
    return pl.pallas_call(
        functools.partial(_final_kernel, alpha=alpha),
        grid=(bsz, seq // ts),
        in_specs=[
            pl.BlockSpec((1, ts, dm), tok),
            pl.BlockSpec((1, ts, dm), tok),
            pl.BlockSpec((1, ts, NSA_WIDTH), tok),
            pl.BlockSpec(wout_c.shape, lambda b, s: (0, 0)),
            pl.BlockSpec(ln_g.shape, lambda b, s: (0, 0)),
            pl.BlockSpec(ln_b.shape, lambda b, s: (0, 0)),
        ],
        out_specs=pl.BlockSpec((1, ts, dm), tok),
        out_shape=jax.ShapeDtypeStruct((bsz, seq, dm), F32),
        compiler_params=pltpu.CompilerParams(
            dimension_semantics=("arbitrary", "arbitrary"), vmem_limit_bytes=VMEM_LIMIT),
        name="final",
    )(x, part, yc, wout_c, ln_g, ln_b)


def _pack_w_in(w_in):
    depth, dm, _ = w_in.shape
    gate0 = COL_C + 2 * NSA_WIDTH + 6 * KV_WIDTH
    per_group = N_GATES // NSA_KV
    gates = w_in[:, :, gate0:gate0 + N_GATES]
    gate_blocks = []
    for g in range(NSA_KV):
        blk = gates[:, :, g * per_group:(g + 1) * per_group]
        gate_blocks.append(jnp.pad(blk, ((0, 0), (0, 0), (0, LANES - per_group))))
    d_cols = w_in[:, :, gate0 + N_GATES:]
    return jnp.concatenate([w_in[:, :, :gate0]] + gate_blocks + [d_cols], axis=-1).astype(BF16)


def _block_diag_gates(wa, wx):
    depth = wa.shape[0]
    per_half = LRU_BLOCKS // 2
    half = per_half * LRU_BLOCK
    halves = []
    for hf in range(2):
        mats = []
        for w in (wa, wx):
            m = jnp.zeros((depth, half, half), F32)
            for j in range(per_half):
                m = m.at[:, j * LRU_BLOCK:(j + 1) * LRU_BLOCK, j * LRU_BLOCK:(j + 1) * LRU_BLOCK].set(
                    w[:, hf * per_half + j])
            mats.append(m)
        halves.append(jnp.concatenate(mats, axis=-1))
    return jnp.stack(halves, axis=1).astype(BF16)


def _overlap_matrix(n_cmp_pad, n_cmp, n_slc):
    ci = np.arange(n_cmp_pad)[:, None] * CMP_STRIDE
    sj = np.arange(SEL_LANES)[None, :] * SLC_BLOCK
    ov = (ci < sj + SLC_BLOCK) & (ci + CMP_LEN > sj)
    ov &= (np.arange(n_cmp_pad)[:, None] < n_cmp) & (np.arange(SEL_LANES)[None, :] < n_slc)
    return jnp.asarray(ov.astype(np.float32), dtype=BF16)


def kernel(x, mem, w_in, sgu_ln_g, sgu_ln_b, sgu_w, sgu_b, conv_w, conv_b, lru_wa, lru_ba, lru_wx, lru_bx,
           lru_lambda, cmp_pos, cmp_w1, cmp_w2, w_mem_kv, w_out, ln_g, ln_b):
    depth = w_in.shape[0]
    bsz, seq, dm = x.shape
    assert seq % TK == 0 and seq // SLC_BLOCK <= SEL_LANES
    alpha = (2 * depth) ** 0.25
    n_chunk = seq // CMP_STRIDE
    n_cmp = (seq - CMP_LEN) // CMP_STRIDE + 1

    w_in_p = _pack_w_in(w_in)
    wbd = _block_diag_gates(lru_wa, lru_wx)
    wmem = w_mem_kv.astype(BF16)
    c0 = A_WIDTH + LRU_WIDTH
    wout_abd = jnp.concatenate([w_out[:, :c0], w_out[:, c0 + NSA_WIDTH:]], axis=1).astype(BF16)
    wout_c = w_out[:, c0:c0 + NSA_WIDTH].astype(BF16)
    w1 = cmp_w1.astype(BF16)
    w2 = cmp_w2.astype(BF16)
    pos_flat = cmp_pos.reshape(depth, 2, 1, CMP_LEN * HEAD_DIM)
    overlap = _overlap_matrix(n_chunk, n_cmp, seq // SLC_BLOCK)
    row = lambda a: a[:, None, :]

    for l in range(depth):
        part, q, zc, gl, kvc_raw, ksa, vs, kw, vw = _proj_mix(
            x, mem, w_in_p[l], row(sgu_ln_g)[l], row(sgu_ln_b)[l], sgu_w[l], sgu_b[l][:, :, None],
            conv_w[l], row(conv_b)[l], wbd[l], row(lru_ba)[l], row(lru_bx)[l], row(lru_lambda)[l],
            wmem[l], wout_abd[l])
        fc = kvc_raw.reshape(bsz, 2, NSA_KV, n_chunk, CMP_STRIDE * HEAD_DIM)
        kcv = _compress(fc, pos_flat[l], w1[l], w2[l])
        yc = _nsa(q, zc, gl, kcv, ksa, vs, kw, vw, overlap)
        x = _final(x, part, yc, wout_c[l], row(ln_g)[l], row(ln_b)[l], alpha)
    return x
```

```python
import functools

import jax
import jax.numpy as jnp
import numpy as np
from jax import lax
from jax.experimental import pallas as pl
from jax.experimental.pallas import tpu as pltpu

F32 = jnp.float32
BF16 = jnp.bfloat16

CHUNK = 128
A_GROUPS = 4
A_WIDTH = 512
LRU_WIDTH = 512
LRU_BLOCKS = 8
LRU_BLOCK = 64
CONV_W = 4
C_LRU = 8.0
HEAD_DIM = 64
NSA_KV = 2
NSA_GROUP = 4
NSA_WIDTH = 512
KV_WIDTH = 128
CMP_LEN = 32
CMP_STRIDE = 16
CMP_HIDDEN = 128
SLC_BLOCK = 64
N_SELECT = 16
WINDOW = 512
N_GATES = 24
MEM_HEADS = 4
MEM_WIDTH = 256
LN_EPS = 1e-5
NEG = -1e30
FORCE_BONUS = 1e4

LANES = 128
SEL_LANES = 128
VMEM_LIMIT = 56 * 1024 * 1024

COL_A = 0
COL_B = 1536
COL_C = 2560
COL_G = 4352
COL_D = 4608
D_IN_P = 5120

TS_PROJ = 256
TQ = 512
TK = 512

Q_SCALE = HEAD_DIM ** -0.5 * float(np.log2(np.e))


def _sigmoid(x):
    return 1.0 / (1.0 + jnp.exp(-x))


def _silu(x):
    return x * _sigmoid(x)


def _gelu(x):
    return jax.nn.gelu(x, approximate=True)


def _dot(a, b):
    return jnp.dot(a, b, preferred_element_type=F32)


def _dot_nt(a, b):
    return lax.dot_general(a, b, (((1,), (1,)), ((), ())), preferred_element_type=F32)


def _proj_mix_kernel(x_ref, mem_ref, w_in_ref, sln_g_ref, sln_b_ref, sw_ref, sb_ref, cw_ref, cb_ref,
                     wbd_ref, ba_ref, bx_ref, lam_ref, wmem_ref, wout_ref,
                     part_ref, q_ref, zc_ref, gl_ref, kvc_ref, ksa_ref, vs_ref, kw_ref, vw_ref,
                     conv_buf, h_carry, memk, memv):
    s = pl.program_id(1)
    ts = x_ref.shape[1]
    xb = x_ref[0].astype(BF16)

    @pl.when(s == 0)
    def _():
        conv_buf[0:8, :] = jnp.zeros((8, LRU_WIDTH), F32)
        h_carry[...] = jnp.zeros_like(h_carry)
        kv = _dot(mem_ref[0].astype(BF16), wmem_ref[...])
        memk[...] = kv[:, :MEM_WIDTH].astype(BF16)
        memv[...] = kv[:, MEM_WIDTH:].astype(BF16)

    pa = _dot(xb, w_in_ref[:, COL_A:COL_A + 3 * A_WIDTH])
    u = _gelu(pa[:, 0:A_WIDTH])
    v = _gelu(pa[:, A_WIDTH:2 * A_WIDTH])
    za = pa[:, 2 * A_WIDTH:3 * A_WIDTH]
    row = lax.broadcasted_iota(jnp.int32, (CHUNK, CHUNK), 0)
    col = lax.broadcasted_iota(jnp.int32, (CHUNK, CHUNK), 1)
    causal = col <= row
    sv_groups = []
    for g in range(A_GROUPS):
        vg = v[:, g * CHUNK:(g + 1) * CHUNK]
        mu = jnp.mean(vg, axis=-1, keepdims=True)
        var = jnp.mean(jnp.square(vg - mu), axis=-1, keepdims=True)
        vn = (vg - mu) * lax.rsqrt(var + LN_EPS)
        vn = vn * sln_g_ref[:, g * CHUNK:(g + 1) * CHUNK] + sln_b_ref[:, g * CHUNK:(g + 1) * CHUNK]
        vnb = vn.astype(BF16)
        wg = jnp.where(causal, sw_ref[g], 0.0).astype(BF16)
        bias = sb_ref[g]
        chunks = []
        for c in range(ts // CHUNK):
            chunks.append(_dot(wg, vnb[c * CHUNK:(c + 1) * CHUNK]) + bias)
        sv_groups.append(jnp.concatenate(chunks, axis=0))
    sv = jnp.concatenate(sv_groups, axis=1)
    ya = (u * sv) * _silu(za)

    pb = _dot(xb, w_in_ref[:, COL_B:COL_B + 2 * LRU_WIDTH])
    xbv = pb[:, :LRU_WIDTH]
    zb = pb[:, LRU_WIDTH:]
    conv_buf[8:8 + ts, :] = xbv
    xc = cb_ref[...] + conv_buf[5:5 + ts, :] * cw_ref[0:1, :]
    xc = xc + conv_buf[6:6 + ts, :] * cw_ref[1:2, :]
    xc = xc + conv_buf[7:7 + ts, :] * cw_ref[2:3, :]
    xc = xc + xbv * cw_ref[3:4, :]
    conv_buf[0:8, :] = xbv[ts - 8:ts, :]
    xcb = xc.astype(BF16)
    half = LRU_WIDTH // 2
    g0 = _dot(xcb[:, :half], wbd_ref[0])
    g1 = _dot(xcb[:, half:], wbd_ref[1])
    r = _sigmoid(jnp.concatenate([g0[:, :half], g1[:, :half]], axis=1) + ba_ref[...])
    ig = _sigmoid(jnp.concatenate([g0[:, half:], g1[:, half:]], axis=1) + bx_ref[...])
    nlam = -lam_ref[...]
    softplus = jnp.maximum(nlam, 0.0) + jnp.log1p(jnp.exp(-jnp.abs(nlam)))
    log_a = (-C_LRU * r) * softplus
    a_cum = jnp.exp(log_a)
    th = jnp.tanh(log_a)
    b_cum = jnp.sqrt((-2.0 * th) / (1.0 - th)) * (ig * xc)
    d = 1
    while d < ts:
        a_sh = jnp.concatenate([jnp.ones((d, LRU_WIDTH), F32), a_cum[:ts - d]], axis=0)
        b_sh = jnp.concatenate([jnp.zeros((d, LRU_WIDTH), F32), b_cum[:ts - d]], axis=0)
        b_cum = b_cum + a_cum * b_sh
        a_cum = a_cum * a_sh
        d *= 2
    h = b_cum + a_cum * h_carry[...]
    h_carry[...] = h[ts - 1:ts, :]
    yb = h * _silu(zb)

    pd = _dot(xb, w_in_ref[:, COL_D:COL_D + 2 * MEM_WIDTH])
    qd = pd[:, :MEM_WIDTH] * (HEAD_DIM ** -0.5)
    zd = pd[:, MEM_WIDTH:]
    head_of_lane = lax.broadcasted_iota(jnp.int32, (1, MEM_WIDTH), 1) // HEAD_DIM
    od = jnp.zeros((ts, MEM_WIDTH), F32)
    for hd in range(MEM_HEADS):
        hm = head_of_lane == hd
        qh = jnp.where(hm, qd, 0.0).astype(BF16)
        sc = _dot_nt(qh, memk[...])
        m = jnp.max(sc, axis=-1, keepdims=True)
        p = jnp.exp(sc - m)
        l = jnp.sum(p, axis=-1, keepdims=True)
        pv = _dot(p.astype(BF16), memv[...])
        od = od + jnp.where(hm, pv * (1.0 / l), 0.0)
    yd = od * _silu(zd)

    y_abd = jnp.concatenate([ya, yb, yd], axis=1).astype(BF16)
    part_ref[0] = _dot(y_abd, wout_ref[...])

    pc = _dot(xb, w_in_ref[:, COL_C:COL_C + 2 * NSA_WIDTH + 6 * KV_WIDTH])
    q_ref[0] = (pc[:, :NSA_WIDTH] * Q_SCALE).astype(BF16)
    zc_ref[0] = pc[:, NSA_WIDTH:2 * NSA_WIDTH]
    gl_ref[0] = _dot(xb, w_in_ref[:, COL_G:COL_G + NSA_KV * LANES])
    base = 2 * NSA_WIDTH
    kpos = s * ts + lax.broadcasted_iota(jnp.int32, (ts, SEL_LANES), 0)
    blk = lax.broadcasted_iota(jnp.int32, (ts, SEL_LANES), 1)
    sel_rows = jnp.where((kpos // SLC_BLOCK) == blk, -NEG, 0.0).astype(BF16)
    ones_col = jnp.where(lax.broadcasted_iota(jnp.int32, (ts, HEAD_DIM), 1) == 0, 1.0, 0.0).astype(BF16)
    for g in range(NSA_KV):
        lo = g * HEAD_DIM
        kvc_ref[0, 0, g] = pc[:, base + lo:base + lo + HEAD_DIM]
        kvc_ref[0, 1, g] = pc[:, base + KV_WIDTH + lo:base + KV_WIDTH + lo + HEAD_DIM]
        ksa_ref[0, g, :, 0:SEL_LANES] = sel_rows
        ksa_ref[0, g, :, SEL_LANES:SEL_LANES + HEAD_DIM] = (
            pc[:, base + 2 * KV_WIDTH + lo:base + 2 * KV_WIDTH + lo + HEAD_DIM].astype(BF16))
        vs_ref[0, g, :, 0:HEAD_DIM] = (
            pc[:, base + 3 * KV_WIDTH + lo:base + 3 * KV_WIDTH + lo + HEAD_DIM].astype(BF16))
        vs_ref[0, g, :, HEAD_DIM:2 * HEAD_DIM] = ones_col
        kw_ref[0, g] = pc[:, base + 4 * KV_WIDTH + lo:base + 4 * KV_WIDTH + lo + HEAD_DIM].astype(BF16)
        vw_ref[0, g, :, 0:HEAD_DIM] = (
            pc[:, base + 5 * KV_WIDTH + lo:base + 5 * KV_WIDTH + lo + HEAD_DIM].astype(BF16))
        vw_ref[0, g, :, HEAD_DIM:2 * HEAD_DIM] = ones_col


def _proj_mix(x, mem, w_in_p, sln_g, sln_b, sw, sb, cw, cb, wbd, ba, bx, lam, wmem, wout_abd):
    bsz, seq, dm = x.shape
    ts = TS_PROJ
    n_mem = mem.shape[1]
    const2 = lambda b, s: (0, 0)
    const3 = lambda b, s: (0, 0, 0)
    in_specs = [
        pl.BlockSpec((1, ts, dm), lambda b, s: (b, s, 0)),
        pl.BlockSpec((1, n_mem, dm), lambda b, s: (b, 0, 0)),
        pl.BlockSpec(w_in_p.shape, const2),
        pl.BlockSpec(sln_g.shape, const2),
        pl.BlockSpec(sln_b.shape, const2),
        pl.BlockSpec(sw.shape, const3),
        pl.BlockSpec(sb.shape, const3),
        pl.BlockSpec(cw.shape, const2),
        pl.BlockSpec(cb.shape, const2),
        pl.BlockSpec(wbd.shape, const3),
        pl.BlockSpec(ba.shape, const2),
        pl.BlockSpec(bx.shape, const2),
        pl.BlockSpec(lam.shape, const2),
        pl.BlockSpec(wmem.shape, const2),
        pl.BlockSpec(wout_abd.shape, const2),
    ]
    kv_spec = pl.BlockSpec((1, NSA_KV, ts, HEAD_DIM), lambda b, s: (b, 0, s, 0))
    va_spec = pl.BlockSpec((1, NSA_KV, ts, 2 * HEAD_DIM), lambda b, s: (b, 0, s, 0))
    out_specs = [
        pl.BlockSpec((1, ts, dm), lambda b, s: (b, s, 0)),
        pl.BlockSpec((1, ts, NSA_WIDTH), lambda b, s: (b, s, 0)),
        pl.BlockSpec((1, ts, NSA_WIDTH), lambda b, s: (b, s, 0)),
        pl.BlockSpec((1, ts, NSA_KV * LANES), lambda b, s: (b, s, 0)),
        pl.BlockSpec((1, 2, NSA_KV, ts, HEAD_DIM), lambda b, s: (b, 0, 0, s, 0)),
        pl.BlockSpec((1, NSA_KV, ts, SEL_LANES + HEAD_DIM), lambda b, s: (b, 0, s, 0)),
        va_spec, kv_spec, va_spec,
    ]
    out_shape = [
        jax.ShapeDtypeStruct((bsz, seq, dm), F32),
        jax.ShapeDtypeStruct((bsz, seq, NSA_WIDTH), BF16),
        jax.ShapeDtypeStruct((bsz, seq, NSA_WIDTH), F32),
        jax.ShapeDtypeStruct((bsz, seq, NSA_KV * LANES), F32),
        jax.ShapeDtypeStruct((bsz, 2, NSA_KV, seq, HEAD_DIM), F32),
        jax.ShapeDtypeStruct((bsz, NSA_KV, seq, SEL_LANES + HEAD_DIM), BF16),
        jax.ShapeDtypeStruct((bsz, NSA_KV, seq, 2 * HEAD_DIM), BF16),
        jax.ShapeDtypeStruct((bsz, NSA_KV, seq, HEAD_DIM), BF16),
        jax.ShapeDtypeStruct((bsz, NSA_KV, seq, 2 * HEAD_DIM), BF16),
    ]
    return pl.pallas_call(
        _proj_mix_kernel,
        grid=(bsz, seq // ts),
        in_specs=in_specs,
        out_specs=out_specs,
        out_shape=out_shape,
        scratch_shapes=[
            pltpu.VMEM((ts + 8, LRU_WIDTH), F32),
            pltpu.VMEM((1, LRU_WIDTH), F32),
            pltpu.VMEM((n_mem, MEM_WIDTH), BF16),
            pltpu.VMEM((n_mem, MEM_WIDTH), BF16),
        ],
        compiler_params=pltpu.CompilerParams(
            dimension_semantics=("arbitrary", "arbitrary"), vmem_limit_bytes=VMEM_LIMIT),
        name="proj_mix",
    )(x, mem, w_in_p, sln_g, sln_b, sw, sb, cw, cb, wbd, ba, bx, lam, wmem, wout_abd)


def _compress_kernel(fc_ref, pos_ref, w1_ref, w2_ref, out_ref):
    fc = fc_ref[0, 0, 0]
    half = (CMP_LEN // 2) * HEAD_DIM
    pos = pos_ref[0]
    fa = (fc + pos[:, :half]).astype(BF16)
    fb = (fc + pos[:, half:]).astype(BF16)
    ha = _dot(fa, w1_ref[0, :half, :])
    hb = _dot(fb, w1_ref[0, half:, :])
    hb_next = jnp.concatenate([hb[1:], hb[:1]], axis=0)
    hid = _gelu(ha + hb_next)
    out_ref[0, 0, 0] = _dot(hid.astype(BF16), w2_ref[0]).astype(BF16)


def _compress(fc, pos_flat, w1, w2):
    bsz, _, _, n_chunk, width = fc.shape
    return pl.pallas_call(
        _compress_kernel,
        grid=(bsz, 2, NSA_KV),
        in_specs=[
            pl.BlockSpec((1, 1, 1, n_chunk, width), lambda b, k, g: (b, k, g, 0, 0)),
            pl.BlockSpec((1, 1, CMP_LEN * HEAD_DIM), lambda b, k, g: (k, 0, 0)),
            pl.BlockSpec((1, CMP_LEN * HEAD_DIM, CMP_HIDDEN), lambda b, k, g: (k, 0, 0)),
            pl.BlockSpec((1, CMP_HIDDEN, HEAD_DIM), lambda b, k, g: (k, 0, 0)),
        ],
        out_specs=pl.BlockSpec((1, 1, 1, n_chunk, HEAD_DIM), lambda b, k, g: (b, k, g, 0, 0)),
        out_shape=jax.ShapeDtypeStruct((bsz, 2, NSA_KV, n_chunk, HEAD_DIM), BF16),
        compiler_params=pltpu.CompilerParams(
            dimension_semantics=("arbitrary", "arbitrary", "arbitrary"), vmem_limit_bytes=VMEM_LIMIT),
        name="compress",
    )(fc, pos_flat, w1, w2)


def _nsa_kernel(q_ref, zc_ref, gl_ref, kcv_ref, ksa_ref, vs_ref, kw_ref, vw_ref, ovt_ref, y_ref, s_buf, *, n_sel):
    qi = pl.program_id(2)
    tq = q_ref.shape[1]
    rows = NSA_GROUP * tq
    s0 = qi * tq
    q = q_ref[0]
    q_heads = [q[:, h * HEAD_DIM:(h + 1) * HEAD_DIM] for h in range(NSA_GROUP)]
    qs = jnp.concatenate(q_heads, axis=0)
    t_q = s0 + lax.broadcasted_iota(jnp.int32, (tq, 1), 0)
    t_row = s0 + (lax.broadcasted_iota(jnp.int32, (rows, 1), 0) & (tq - 1))

    kc = kcv_ref[0, 0, 0]
    vc = kcv_ref[0, 1, 0]
    n_cmp = kc.shape[0]
    cmp_end = lax.broadcasted_iota(jnp.int32, (1, n_cmp), 1) * CMP_STRIDE + (CMP_LEN - 1)
    valid_c = cmp_end <= t_q
    o_c = []
    ps = None
    for h in range(NSA_GROUP):
        sc = jnp.where(valid_c, _dot_nt(q_heads[h], kc), NEG)
        m = jnp.max(sc, axis=-1, keepdims=True)
        p = jnp.where(valid_c, jnp.exp2(sc - m), 0.0)
        l = jnp.sum(p, axis=-1, keepdims=True)
        pn = p * jnp.where(l > 0.0, 1.0 / l, 0.0)
        o_c.append(_dot(pn.astype(BF16), vc))
        ps = pn if ps is None else ps + pn

    ps_hi = ps.astype(BF16)
    ps_lo = (ps - ps_hi.astype(F32)).astype(BF16)
    imp = _dot_nt(ovt_ref[...], ps_hi) + _dot_nt(ovt_ref[...], ps_lo)
    cur = (s0 + lax.broadcasted_iota(jnp.int32, (1, tq), 1)) // SLC_BLOCK
    blk = lax.broadcasted_iota(jnp.int32, (SEL_LANES, 1), 0)
    future = blk > cur
    forced = (blk == 0) | (blk == cur) | (blk == cur - 1)
    work = jnp.where(future, -1.0, imp + jnp.where(forced, FORCE_BONUS, 0.0))
    sel_t = jnp.full((SEL_LANES, tq), -1.0, F32)
    blk_f = blk.astype(F32)
    for _ in range(n_sel):
        mx = jnp.max(work, axis=0, keepdims=True)
        idx = jnp.min(jnp.where(work == mx, blk_f, float(SEL_LANES)), axis=0, keepdims=True)
        hit = blk_f == idx
        sel_t = jnp.where(hit, 0.0, sel_t)
        work = jnp.where(hit, -jnp.inf, work)
    sel_rows = jnp.concatenate([sel_t.T.astype(BF16)] * NSA_GROUP, axis=0)
    q_aug = jnp.concatenate([sel_rows, qs], axis=1)

    def online_step(s, v, carry, valid):
        m_i, acc = carry
        s = jnp.where(valid, s, NEG)
        m_new = jnp.maximum(m_i, jnp.max(s, axis=-1, keepdims=True))
        p_t = jnp.exp2(s - m_new)
        return m_new, jnp.exp2(m_i - m_new) * acc + _dot(p_t.astype(BF16), v)

    n_tiles = ksa_ref.shape[2] // TK
    key_lane = lax.broadcasted_iota(jnp.int32, (1, TK), 1)

    def tile_start(kt):
        return pl.multiple_of(jnp.clip(kt, 0, n_tiles - 1) * TK, TK)

    def run_tiles(q_op, k_ref, v_ref, kt0, stride, n, valid_fn):
        def scores(i):
            return _dot_nt(q_op, k_ref[0, 0, pl.ds(tile_start(kt0 + stride * i), TK), :])

        def step(i, slot, carry, prefetch):
            kt = kt0 + stride * i
            s = s_buf[slot]
            if prefetch:
                s_buf[1 - slot] = scores(i + 1)
            v = v_ref[0, 0, pl.ds(tile_start(kt), TK), :]
            return online_step(s, v, carry, valid_fn(kt))

        def group(width, first):
            def body(j, carry):
                for i in range(width):
                    carry = step(first + width * j + i, i % 2, carry, True)
                return carry
            return body

        s_buf[0] = scores(0)
        carry = (jnp.full((rows, 1), NEG, F32), jnp.zeros((rows, 2 * HEAD_DIM), F32))
        n4 = n // 4
        carry = lax.fori_loop(0, n4, group(4, 0), carry)
        carry = lax.fori_loop(0, (n % 4) // 2, group(2, 4 * n4), carry)
        carry = lax.fori_loop(0, n % 2, lambda _, c: step(n - 1, 0, c, False), carry)
        acc = carry[1]
        return acc[:, :HEAD_DIM] * (1.0 / acc[:, HEAD_DIM:HEAD_DIM + 1])

    o_s = run_tiles(q_aug, ksa_ref, vs_ref, 0, 1, (tq // TK) * (qi + 1),
                    lambda kt: (kt * TK + key_lane) <= t_row)

    def win_valid(kt):
        base = jnp.where(kt >= 0, kt * TK, -(1 << 24))
        return (t_row - (base + key_lane)).astype(jnp.uint32) < jnp.uint32(WINDOW)

    n_win = (WINDOW + tq) // TK + jnp.minimum(qi, 0)
    o_w = run_tiles(qs, kw_ref, vw_ref, qi, -1, n_win, win_valid)

    gates = _sigmoid(gl_ref[0])
    heads = []
    for h in range(NSA_GROUP):
        y_h = o_c[h] * gates[:, 3 * h:3 * h + 1]
        y_h = y_h + o_s[h * tq:(h + 1) * tq] * gates[:, 3 * h + 1:3 * h + 2]
        y_h = y_h + o_w[h * tq:(h + 1) * tq] * gates[:, 3 * h + 2:3 * h + 3]
        heads.append(y_h)
    y = jnp.concatenate(heads, axis=1)
    y_ref[0] = (y * _silu(zc_ref[0])).astype(BF16)


def _nsa(q, zc, gl, kcv, ksa, vs, kw, vw, overlap):
    bsz, seq, _ = q.shape
    n_cmp = kcv.shape[3]
    gw = NSA_GROUP * HEAD_DIM
    n_sel = min(N_SELECT, seq // SLC_BLOCK)
    per_bg = lambda b, g, i: (b, g, 0, 0)
    return pl.pallas_call(
        functools.partial(_nsa_kernel, n_sel=n_sel),
        grid=(bsz, NSA_KV, seq // TQ),
        in_specs=[
            pl.BlockSpec((1, TQ, gw), lambda b, g, i: (b, i, g)),
            pl.BlockSpec((1, TQ, gw), lambda b, g, i: (b, i, g)),
            pl.BlockSpec((1, TQ, LANES), lambda b, g, i: (b, i, g)),
            pl.BlockSpec((1, 2, 1, n_cmp, HEAD_DIM), lambda b, g, i: (b, 0, g, 0, 0)),
            pl.BlockSpec((1, 1, seq, SEL_LANES + HEAD_DIM), per_bg, pipeline_mode=pl.Buffered(1)),
            pl.BlockSpec((1, 1, seq, 2 * HEAD_DIM), per_bg, pipeline_mode=pl.Buffered(1)),
            pl.BlockSpec((1, 1, seq, HEAD_DIM), per_bg, pipeline_mode=pl.Buffered(1)),
            pl.BlockSpec((1, 1, seq, 2 * HEAD_DIM), per_bg, pipeline_mode=pl.Buffered(1)),
            pl.BlockSpec(overlap.shape, lambda b, g, i: (0, 0), pipeline_mode=pl.Buffered(1)),
        ],
        out_specs=pl.BlockSpec((1, TQ, gw), lambda b, g, i: (b, i, g)),
        out_shape=jax.ShapeDtypeStruct((bsz, seq, NSA_WIDTH), BF16),
        scratch_shapes=[pltpu.VMEM((2, NSA_GROUP * TQ, TK), F32)],
        compiler_params=pltpu.CompilerParams(
            dimension_semantics=("arbitrary", "arbitrary", "arbitrary"), vmem_limit_bytes=VMEM_LIMIT),
        name="nsa",
    )(q, zc, gl, kcv, ksa, vs, kw, vw, overlap)


def _final_kernel(x_ref, part_ref, yc_ref, wout_ref, g_ref, b_ref, o_ref, *, alpha):
    out = part_ref[0] + _dot(yc_ref[0], wout_ref[...])
    y = alpha * x_ref[0] + out
    mu = jnp.mean(y, axis=-1, keepdims=True)
    var = jnp.mean(jnp.square(y - mu), axis=-1, keepdims=True)
    o_ref[0] = ((y - mu) * lax.rsqrt(var + LN_EPS)) * g_ref[...] + b_ref[...]


def _final(x, part, yc, wout_c, ln_g, ln_b, alpha):
    bsz, seq, dm = x.shape
    ts = TS_PROJ
    tok = lambda b, s: (b, s, 0)
    return pl.pallas_call(
        functools.partial(_final_kernel, alpha=alpha),
        grid=(bsz, seq // ts),
        in_specs=[
            pl.BlockSpec((1, ts, dm), tok),
            pl.BlockSpec((1, ts, dm), tok),
            pl.BlockSpec((1, ts, NSA_WIDTH), tok),
            pl.BlockSpec(wout_c.shape, lambda b, s: (0, 0)),
            pl.BlockSpec(ln_g.shape, lambda b, s: (0, 0)),
            pl.BlockSpec(ln_b.shape, lambda b, s: (0, 0)),
        ],
        out_specs=pl.BlockSpec((1, ts, dm), tok),
        out_shape=jax.ShapeDtypeStruct((bsz, seq, dm), F32),
        compiler_params=pltpu.CompilerParams(
            dimension_semantics=("arbitrary", "arbitrary"), vmem_limit_bytes=VMEM_LIMIT),
        name="final",
    )(x, part, yc, wout_c, ln_g, ln_b)


def _pack_w_in(w_in):
    depth, dm, _ = w_in.shape
    gate0 = COL_C + 2 * NSA_WIDTH + 6 * KV_WIDTH
    per_group = N_GATES // NSA_KV
    gates = w_in[:, :, gate0:gate0 + N_GATES]
    gate_blocks = []
    for g in range(NSA_KV):
        blk = gates[:, :, g * per_group:(g + 1) * per_group]
        gate_blocks.append(jnp.pad(blk, ((0, 0), (0, 0), (0, LANES - per_group))))
    d_cols = w_in[:, :, gate0 + N_GATES:]
    return jnp.concatenate([w_in[:, :, :gate0]] + gate_blocks + [d_cols], axis=-1).astype(BF16)


def _block_diag_gates(wa, wx):
    depth = wa.shape[0]
    per_half = LRU_BLOCKS // 2
    half = per_half * LRU_BLOCK
    halves = []
    for hf in range(2):
        mats = []
        for w in (wa, wx):
            m = jnp.zeros((depth, half, half), F32)
            for j in range(per_half):
                m = m.at[:, j * LRU_BLOCK:(j + 1) * LRU_BLOCK, j * LRU_BLOCK:(j + 1) * LRU_BLOCK].set(
                    w[:, hf * per_half + j])
            mats.append(m)
        halves.append(jnp.concatenate(mats, axis=-1))
    return jnp.stack(halves, axis=1).astype(BF16)


def _overlap_matrix(n_cmp_pad, n_cmp, n_slc):
    ci = np.arange(n_cmp_pad)[:, None] * CMP_STRIDE
    sj = np.arange(SEL_LANES)[None, :] * SLC_BLOCK
    ov = (ci < sj + SLC_BLOCK) & (ci + CMP_LEN > sj)
    ov &= (np.arange(n_cmp_pad)[:, None] < n_cmp) & (np.arange(SEL_LANES)[None, :] < n_slc)
    return jnp.asarray(ov.T.astype(np.float32), dtype=BF16)


def kernel(x, mem, w_in, sgu_ln_g, sgu_ln_b, sgu_w, sgu_b, conv_w, conv_b, lru_wa, lru_ba, lru_wx, lru_bx,
           lru_lambda, cmp_pos, cmp_w1, cmp_w2, w_mem_kv, w_out, ln_g, ln_b):
    depth = w_in.shape[0]
    bsz, seq, dm = x.shape
    assert TQ == TK and WINDOW <= TK
    assert seq % TK == 0 and seq // SLC_BLOCK <= SEL_LANES
    alpha = (2 * depth) ** 0.25
    n_chunk = seq // CMP_STRIDE
    n_cmp = (seq - CMP_LEN) // CMP_STRIDE + 1

    w_in_p = _pack_w_in(w_in)
    wbd = _block_diag_gates(lru_wa, lru_wx)
    wmem = w_mem_kv.astype(BF16)
    c0 = A_WIDTH + LRU_WIDTH
    wout_abd = jnp.concatenate([w_out[:, :c0], w_out[:, c0 + NSA_WIDTH:]], axis=1).astype(BF16)
    wout_c = w_out[:, c0:c0 + NSA_WIDTH].astype(BF16)
    w1 = cmp_w1.astype(BF16)
    w2 = cmp_w2.astype(BF16)
    pos_flat = cmp_pos.reshape(depth, 2, 1, CMP_LEN * HEAD_DIM)
    overlap = _overlap_matrix(n_chunk, n_cmp, seq // SLC_BLOCK)
    row = lambda a: a[:, None, :]

    for l in range(depth):
        part, q, zc, gl, kvc_raw, ksa, vs, kw, vw = _proj_mix(
            x, mem, w_in_p[l], row(sgu_ln_g)[l], row(sgu_ln_b)[l], sgu_w[l], sgu_b[l][:, :, None],
            conv_w[l], row(conv_b)[l], wbd[l], row(lru_ba)[l], row(lru_bx)[l], row(lru_lambda)[l],
            wmem[l], wout_abd[l])
        fc = kvc_raw.reshape(bsz, 2, NSA_KV, n_chunk, CMP_STRIDE * HEAD_DIM)
        kcv = _compress(fc, pos_flat[l], w1[l], w2[l])
        yc = _nsa(q, zc, gl, kcv, ksa, vs, kw, vw, overlap)
        x = _final(x, part, yc, wout_c[l], row(ln_g)[l], row(ln_b)[l], alpha)
    return x
```

```python
import functools

import jax
import jax.numpy as jnp
import numpy as np
from jax import lax
from jax.experimental import pallas as pl
from jax.experimental.pallas import tpu as pltpu

F32 = jnp.float32
BF16 = jnp.bfloat16

CHUNK = 128
A_GROUPS = 4
A_WIDTH = 512
LRU_WIDTH = 512
LRU_BLOCKS = 8
LRU_BLOCK = 64
CONV_W = 4
C_LRU = 8.0
HEAD_DIM = 64
NSA_KV = 2
NSA_GROUP = 4
NSA_WIDTH = 512
KV_WIDTH = 128
CMP_LEN = 32
CMP_STRIDE = 16
CMP_HIDDEN = 128
SLC_BLOCK = 64
N_SELECT = 16
WINDOW = 512
N_GATES = 24
MEM_HEADS = 4
MEM_WIDTH = 256
LN_EPS = 1e-5
NEG = -1e30
FORCE_BONUS = 1e4

LANES = 128
SEL_LANES = 128
VMEM_LIMIT = 56 * 1024 * 1024

COL_A = 0
COL_B = 1536
COL_C = 2560
COL_G = 4352
COL_D = 4608
D_IN_P = 5120

TS_PROJ = 256
TQ = 512
TK = 512

Q_SCALE = HEAD_DIM ** -0.5 * float(np.log2(np.e))


def _sigmoid(x):
    return 1.0 / (1.0 + jnp.exp(-x))


def _silu(x):
    return x * _sigmoid(x)


def _gelu(x):
    return jax.nn.gelu(x, approximate=True)


def _dot(a, b):
    return jnp.dot(a, b, preferred_element_type=F32)


def _dot_nt(a, b):
    return lax.dot_general(a, b, (((1,), (1,)), ((), ())), preferred_element_type=F32)


def _proj_mix_kernel(x_ref, mem_ref, w_in_ref, sln_g_ref, sln_b_ref, sw_ref, sb_ref, cw_ref, cb_ref,
                     wbd_ref, ba_ref, bx_ref, lam_ref, wmem_ref, wout_ref,
                     part_ref, q_ref, zc_ref, gl_ref, kvc_ref, ksa_ref, vs_ref, kw_ref, vw_ref,
                     conv_buf, h_carry, memk, memv):
    s = pl.program_id(1)
    ts = x_ref.shape[1]
    xb = x_ref[0].astype(BF16)

    @pl.when(s == 0)
    def _():
        conv_buf[0:8, :] = jnp.zeros((8, LRU_WIDTH), F32)
        h_carry[...] = jnp.zeros_like(h_carry)
        kv = _dot(mem_ref[0].astype(BF16), wmem_ref[...])
        memk[...] = kv[:, :MEM_WIDTH].astype(BF16)
        memv[...] = kv[:, MEM_WIDTH:].astype(BF16)

    pa = _dot(xb, w_in_ref[:, COL_A:COL_A + 3 * A_WIDTH])
    u = _gelu(pa[:, 0:A_WIDTH])
    v = _gelu(pa[:, A_WIDTH:2 * A_WIDTH])
    za = pa[:, 2 * A_WIDTH:3 * A_WIDTH]
    row = lax.broadcasted_iota(jnp.int32, (CHUNK, CHUNK), 0)
    col = lax.broadcasted_iota(jnp.int32, (CHUNK, CHUNK), 1)
    causal = col <= row
    sv_groups = []
    for g in range(A_GROUPS):
        vg = v[:, g * CHUNK:(g + 1) * CHUNK]
        mu = jnp.mean(vg, axis=-1, keepdims=True)
        var = jnp.mean(jnp.square(vg - mu), axis=-1, keepdims=True)
        vn = (vg - mu) * lax.rsqrt(var + LN_EPS)
        vn = vn * sln_g_ref[:, g * CHUNK:(g + 1) * CHUNK] + sln_b_ref[:, g * CHUNK:(g + 1) * CHUNK]
        vnb = vn.astype(BF16)
        wg = jnp.where(causal, sw_ref[g], 0.0).astype(BF16)
        bias = sb_ref[g]
        chunks = []
        for c in range(ts // CHUNK):
            chunks.append(_dot(wg, vnb[c * CHUNK:(c + 1) * CHUNK]) + bias)
        sv_groups.append(jnp.concatenate(chunks, axis=0))
    sv = jnp.concatenate(sv_groups, axis=1)
    ya = (u * sv) * _silu(za)

    pb = _dot(xb, w_in_ref[:, COL_B:COL_B + 2 * LRU_WIDTH])
    xbv = pb[:, :LRU_WIDTH]
    zb = pb[:, LRU_WIDTH:]
    conv_buf[8:8 + ts, :] = xbv
    xc = cb_ref[...] + conv_buf[5:5 + ts, :] * cw_ref[0:1, :]
    xc = xc + conv_buf[6:6 + ts, :] * cw_ref[1:2, :]
    xc = xc + conv_buf[7:7 + ts, :] * cw_ref[2:3, :]
    xc = xc + xbv * cw_ref[3:4, :]
    conv_buf[0:8, :] = xbv[ts - 8:ts, :]
    xcb = xc.astype(BF16)
    half = LRU_WIDTH // 2
    g0 = _dot(xcb[:, :half], wbd_ref[0])
    g1 = _dot(xcb[:, half:], wbd_ref[1])
    r = _sigmoid(jnp.concatenate([g0[:, :half], g1[:, :half]], axis=1) + ba_ref[...])
    ig = _sigmoid(jnp.concatenate([g0[:, half:], g1[:, half:]], axis=1) + bx_ref[...])
    nlam = -lam_ref[...]
    softplus = jnp.maximum(nlam, 0.0) + jnp.log1p(jnp.exp(-jnp.abs(nlam)))
    log_a = (-C_LRU * r) * softplus
    a_cum = jnp.exp(log_a)
    th = jnp.tanh(log_a)
    b_cum = jnp.sqrt((-2.0 * th) / (1.0 - th)) * (ig * xc)
    d = 1
    while d < ts:
        a_sh = jnp.concatenate([jnp.ones((d, LRU_WIDTH), F32), a_cum[:ts - d]], axis=0)
        b_sh = jnp.concatenate([jnp.zeros((d, LRU_WIDTH), F32), b_cum[:ts - d]], axis=0)
        b_cum = b_cum + a_cum * b_sh
        a_cum = a_cum * a_sh
        d *= 2
    h = b_cum + a_cum * h_carry[...]
    h_carry[...] = h[ts - 1:ts, :]
    yb = h * _silu(zb)

    pd = _dot(xb, w_in_ref[:, COL_D:COL_D + 2 * MEM_WIDTH])
    qd = pd[:, :MEM_WIDTH] * (HEAD_DIM ** -0.5)
    zd = pd[:, MEM_WIDTH:]
    head_of_lane = lax.broadcasted_iota(jnp.int32, (1, MEM_WIDTH), 1) // HEAD_DIM
    od = jnp.zeros((ts, MEM_WIDTH), F32)
    for hd in range(MEM_HEADS):
        hm = head_of_lane == hd
        qh = jnp.where(hm, qd, 0.0).astype(BF16)
        sc = _dot_nt(qh, memk[...])
        m = jnp.max(sc, axis=-1, keepdims=True)
        p = jnp.exp(sc - m)
        l = jnp.sum(p, axis=-1, keepdims=True)
        pv = _dot(p.astype(BF16), memv[...])
        od = od + jnp.where(hm, pv * (1.0 / l), 0.0)
    yd = od * _silu(zd)

    y_abd = jnp.concatenate([ya, yb, yd], axis=1).astype(BF16)
    part_ref[0] = _dot(y_abd, wout_ref[...])

    pc = _dot(xb, w_in_ref[:, COL_C:COL_C + 2 * NSA_WIDTH + 6 * KV_WIDTH])
    q_ref[0] = (pc[:, :NSA_WIDTH] * Q_SCALE).astype(BF16)
    zc_ref[0] = pc[:, NSA_WIDTH:2 * NSA_WIDTH]
    gl_ref[0] = _dot(xb, w_in_ref[:, COL_G:COL_G + NSA_KV * LANES])
    base = 2 * NSA_WIDTH
    kpos = s * ts + lax.broadcasted_iota(jnp.int32, (ts, SEL_LANES), 0)
    blk = lax.broadcasted_iota(jnp.int32, (ts, SEL_LANES), 1)
    sel_rows = jnp.where((kpos // SLC_BLOCK) == blk, -NEG, 0.0).astype(BF16)
    ones_col = jnp.where(lax.broadcasted_iota(jnp.int32, (ts, HEAD_DIM), 1) == 0, 1.0, 0.0).astype(BF16)
    for g in range(NSA_KV):
        lo = g * HEAD_DIM
        kvc_ref[0, 0, g] = pc[:, base + lo:base + lo + HEAD_DIM]
        kvc_ref[0, 1, g] = pc[:, base + KV_WIDTH + lo:base + KV_WIDTH + lo + HEAD_DIM]
        ksa_ref[0, g, :, 0:SEL_LANES] = sel_rows
        ksa_ref[0, g, :, SEL_LANES:SEL_LANES + HEAD_DIM] = (
            pc[:, base + 2 * KV_WIDTH + lo:base + 2 * KV_WIDTH + lo + HEAD_DIM].astype(BF16))
        vs_ref[0, g, :, 0:HEAD_DIM] = (
            pc[:, base + 3 * KV_WIDTH + lo:base + 3 * KV_WIDTH + lo + HEAD_DIM].astype(BF16))
        vs_ref[0, g, :, HEAD_DIM:2 * HEAD_DIM] = ones_col
        kw_ref[0, g] = pc[:, base + 4 * KV_WIDTH + lo:base + 4 * KV_WIDTH + lo + HEAD_DIM].astype(BF16)
        vw_ref[0, g, :, 0:HEAD_DIM] = (
            pc[:, base + 5 * KV_WIDTH + lo:base + 5 * KV_WIDTH + lo + HEAD_DIM].astype(BF16))
        vw_ref[0, g, :, HEAD_DIM:2 * HEAD_DIM] = ones_col


def _proj_mix(x, mem, w_in_p, sln_g, sln_b, sw, sb, cw, cb, wbd, ba, bx, lam, wmem, wout_abd):
    bsz, seq, dm = x.shape
    ts = TS_PROJ
    n_mem = mem.shape[1]
    const2 = lambda b, s: (0, 0)
    const3 = lambda b, s: (0, 0, 0)
    in_specs = [
        pl.BlockSpec((1, ts, dm), lambda b, s: (b, s, 0)),
        pl.BlockSpec((1, n_mem, dm), lambda b, s: (b, 0, 0)),
        pl.BlockSpec(w_in_p.shape, const2),
        pl.BlockSpec(sln_g.shape, const2),
        pl.BlockSpec(sln_b.shape, const2),
        pl.BlockSpec(sw.shape, const3),
        pl.BlockSpec(sb.shape, const3),
        pl.BlockSpec(cw.shape, const2),
        pl.BlockSpec(cb.shape, const2),
        pl.BlockSpec(wbd.shape, const3),
        pl.BlockSpec(ba.shape, const2),
        pl.BlockSpec(bx.shape, const2),
        pl.BlockSpec(lam.shape, const2),
        pl.BlockSpec(wmem.shape, const2),
        pl.BlockSpec(wout_abd.shape, const2),
    ]
    kv_spec = pl.BlockSpec((1, NSA_KV, ts, HEAD_DIM), lambda b, s: (b, 0, s, 0))
    va_spec = pl.BlockSpec((1, NSA_KV, ts, 2 * HEAD_DIM), lambda b, s: (b, 0, s, 0))
    out_specs = [
        pl.BlockSpec((1, ts, dm), lambda b, s: (b, s, 0)),
        pl.BlockSpec((1, ts, NSA_WIDTH), lambda b, s: (b, s, 0)),
        pl.BlockSpec((1, ts, NSA_WIDTH), lambda b, s: (b, s, 0)),
        pl.BlockSpec((1, ts, NSA_KV * LANES), lambda b, s: (b, s, 0)),
        pl.BlockSpec((1, 2, NSA_KV, ts, HEAD_DIM), lambda b, s: (b, 0, 0, s, 0)),
        pl.BlockSpec((1, NSA_KV, ts, SEL_LANES + HEAD_DIM), lambda b, s: (b, 0, s, 0)),
        va_spec, kv_spec, va_spec,
    ]
    out_shape = [
        jax.ShapeDtypeStruct((bsz, seq, dm), F32),
        jax.ShapeDtypeStruct((bsz, seq, NSA_WIDTH), BF16),
        jax.ShapeDtypeStruct((bsz, seq, NSA_WIDTH), F32),
        jax.ShapeDtypeStruct((bsz, seq, NSA_KV * LANES), F32),
        jax.ShapeDtypeStruct((bsz, 2, NSA_KV, seq, HEAD_DIM), F32),
        jax.ShapeDtypeStruct((bsz, NSA_KV, seq, SEL_LANES + HEAD_DIM), BF16),
        jax.ShapeDtypeStruct((bsz, NSA_KV, seq, 2 * HEAD_DIM), BF16),
        jax.ShapeDtypeStruct((bsz, NSA_KV, seq, HEAD_DIM), BF16),
        jax.ShapeDtypeStruct((bsz, NSA_KV, seq, 2 * HEAD_DIM), BF16),
    ]
    return pl.pallas_call(
        _proj_mix_kernel,
        grid=(bsz, seq // ts),
        in_specs=in_specs,
        out_specs=out_specs,
        out_shape=out_shape,
        scratch_shapes=[
            pltpu.VMEM((ts + 8, LRU_WIDTH), F32),
            pltpu.VMEM((1, LRU_WIDTH), F32),
            pltpu.VMEM((n_mem, MEM_WIDTH), BF16),
            pltpu.VMEM((n_mem, MEM_WIDTH), BF16),
        ],
        compiler_params=pltpu.CompilerParams(
            dimension_semantics=("arbitrary", "arbitrary"), vmem_limit_bytes=VMEM_LIMIT),
        name="proj_mix",
    )(x, mem, w_in_p, sln_g, sln_b, sw, sb, cw, cb, wbd, ba, bx, lam, wmem, wout_abd)


def _compress_kernel(fc_ref, pos_ref, w1_ref, w2_ref, out_ref):
    fc = fc_ref[0, 0, 0]
    half = (CMP_LEN // 2) * HEAD_DIM
    pos = pos_ref[0]
    fa = (fc + pos[:, :half]).astype(BF16)
    fb = (fc + pos[:, half:]).astype(BF16)
    ha = _dot(fa, w1_ref[0, :half, :])
    hb = _dot(fb, w1_ref[0, half:, :])
    hb_next = jnp.concatenate([hb[1:], hb[:1]], axis=0)
    hid = _gelu(ha + hb_next)
    out_ref[0, 0, 0] = _dot(hid.astype(BF16), w2_ref[0]).astype(BF16)


def _compress(fc, pos_flat, w1, w2):
    bsz, _, _, n_chunk, width = fc.shape
    return pl.pallas_call(
        _compress_kernel,
        grid=(bsz, 2, NSA_KV),
        in_specs=[
            pl.BlockSpec((1, 1, 1, n_chunk, width), lambda b, k, g: (b, k, g, 0, 0)),
            pl.BlockSpec((1, 1, CMP_LEN * HEAD_DIM), lambda b, k, g: (k, 0, 0)),
            pl.BlockSpec((1, CMP_LEN * HEAD_DIM, CMP_HIDDEN), lambda b, k, g: (k, 0, 0)),
            pl.BlockSpec((1, CMP_HIDDEN, HEAD_DIM), lambda b, k, g: (k, 0, 0)),
        ],
        out_specs=pl.BlockSpec((1, 1, 1, n_chunk, HEAD_DIM), lambda b, k, g: (b, k, g, 0, 0)),
        out_shape=jax.ShapeDtypeStruct((bsz, 2, NSA_KV, n_chunk, HEAD_DIM), BF16),
        compiler_params=pltpu.CompilerParams(
            dimension_semantics=("arbitrary", "arbitrary", "arbitrary"), vmem_limit_bytes=VMEM_LIMIT),
        name="compress",
    )(fc, pos_flat, w1, w2)


def _nsa_kernel(q_ref, zc_ref, gl_ref, kcv_ref, ksa_ref, vs_ref, kw_ref, vw_ref, ovt_ref, y_ref, s_buf, *, n_sel):
    qi = pl.program_id(2)
    tq = q_ref.shape[1]
    rows = NSA_GROUP * tq
    s0 = qi * tq
    q = q_ref[0]
    q_heads = [q[:, h * HEAD_DIM:(h + 1) * HEAD_DIM] for h in range(NSA_GROUP)]
    qs = jnp.concatenate(q_heads, axis=0)
    t_q = s0 + lax.broadcasted_iota(jnp.int32, (tq, 1), 0)
    t_row = s0 + (lax.broadcasted_iota(jnp.int32, (rows, 1), 0) & (tq - 1))

    kc = kcv_ref[0, 0, 0]
    vc = kcv_ref[0, 1, 0]
    n_cmp = kc.shape[0]
    cmp_end = lax.broadcasted_iota(jnp.int32, (1, n_cmp), 1) * CMP_STRIDE + (CMP_LEN - 1)
    valid_c = cmp_end <= t_q
    any_valid = t_q >= CMP_LEN - 1
    o_c = []
    ps = None
    for h in range(NSA_GROUP):
        sc = jnp.where(valid_c, _dot_nt(q_heads[h], kc), NEG)
        m = jnp.max(sc, axis=-1, keepdims=True)
        p = jnp.exp2(sc - m)
        l = jnp.sum(p, axis=-1, keepdims=True)
        pn = p * jnp.where(any_valid, 1.0 / l, 0.0)
        o_c.append(_dot(pn.astype(BF16), vc))
        ps = pn if ps is None else ps + pn

    n_tiles = ksa_ref.shape[2] // TK
    key_lane = lax.broadcasted_iota(jnp.int32, (1, TK), 1)

    def tile_start(kt):
        return pl.multiple_of(jnp.clip(kt, 0, n_tiles - 1) * TK, TK)

    one = 1 + jnp.minimum(qi, 0)

    def run_tiles(q_op, k_ref, v_ref, kt0, stride, n_plain, n_masked, valid_fn, side_fn=None, side=None):
        def tile(ref, i):
            return ref[0, 0, pl.ds(tile_start(kt0 + stride * i), TK), :]

        def step(i, slot, carry, prefetch, masked):
            (m_i, acc), side_i = carry
            s = s_buf[slot]
            if prefetch:
                s_buf[1 - slot] = _dot_nt(q_op, tile(k_ref, i + 1))
            if masked:
                s = jnp.where(valid_fn(kt0 + stride * i), s, NEG)
            m_new = jnp.maximum(m_i, jnp.max(s, axis=-1, keepdims=True))
            p_t = jnp.exp2(s - m_new).astype(BF16)
            acc = jnp.exp2(m_i - m_new) * acc + _dot(p_t, tile(v_ref, i))
            return (m_new, acc), (side_fn(side_i) if side_fn is not None else side_i)

        def group(width, first, masked, prefetch_last=True):
            def body(j, carry):
                for i in range(width):
                    carry = step(first + width * j + i, i % 2, carry, prefetch_last or i + 1 < width, masked)
                return carry
            return body

        s_buf[0] = _dot_nt(q_op, tile(k_ref, 0))
        carry = ((jnp.full((rows, 1), NEG, F32), jnp.zeros((rows, 2 * HEAD_DIM), F32)), side)
        if isinstance(n_plain, int):
            assert n_plain == 0 and n_masked == 2
            carry = lax.fori_loop(0, one, group(2, 0, True, prefetch_last=False), carry)
        else:
            assert n_masked == 1
            n4 = n_plain // 4
            carry = lax.fori_loop(0, n4, group(4, 0, False), carry)
            carry = lax.fori_loop(0, (n_plain % 4) // 2, group(2, 4 * n4, False), carry)
            carry = lax.fori_loop(0, n_plain % 2, group(1, n_plain - 1, False), carry)
            carry = lax.fori_loop(0, one, lambda _, c: step(n_plain, n_plain % 2, c, False, True), carry)
        acc = carry[0][1]
        return acc[:, :HEAD_DIM] * (1.0 / acc[:, HEAD_DIM:HEAD_DIM + 1]), carry[1]

    ps_hi = ps.astype(BF16)
    ps_lo = (ps - ps_hi.astype(F32)).astype(BF16)
    imp = _dot_nt(ovt_ref[...], ps_hi) + _dot_nt(ovt_ref[...], ps_lo)
    cur = (s0 + lax.broadcasted_iota(jnp.int32, (1, tq), 1)) // SLC_BLOCK
    blk = lax.broadcasted_iota(jnp.int32, (SEL_LANES, 1), 0)
    future = blk > cur
    forced = (blk == 0) | (blk == cur) | (blk == cur - 1)
    work = jnp.where(future, -1.0, imp + jnp.where(forced, FORCE_BONUS, 0.0))
    blk_f = blk.astype(F32)

    def knock_out(work_i, rounds):
        for _ in range(rounds):
            mx = jnp.max(work_i, axis=0, keepdims=True)
            idx = jnp.min(jnp.where(work_i == mx, blk_f, float(SEL_LANES)), axis=0, keepdims=True)
            work_i = jnp.where(blk_f == idx, -jnp.inf, work_i)
        return work_i

    def win_valid(kt):
        base = jnp.where(kt >= 0, kt * TK, -(1 << 24))
        return (t_row - (base + key_lane)).astype(jnp.uint32) < jnp.uint32(WINDOW)

    assert tq == TK and (WINDOW + tq) // TK == 2 and n_sel % 2 == 0
    o_w, work = run_tiles(qs, kw_ref, vw_ref, qi, -1, 0, 2, win_valid,
                          side_fn=lambda w: knock_out(w, n_sel // 2), side=work)
    sel_t = jnp.where(work == -jnp.inf, 0.0, -1.0)
    sel_rows = jnp.concatenate([sel_t.T.astype(BF16)] * NSA_GROUP, axis=0)
    q_aug = jnp.concatenate([sel_rows, qs], axis=1)

    o_s, _ = run_tiles(q_aug, ksa_ref, vs_ref, 0, 1, qi, 1, lambda kt: (kt * TK + key_lane) <= t_row)

    gates = _sigmoid(gl_ref[0])
    heads = []
    for h in range(NSA_GROUP):
        y_h = o_c[h] * gates[:, 3 * h:3 * h + 1]
        y_h = y_h + o_s[h * tq:(h + 1) * tq] * gates[:, 3 * h + 1:3 * h + 2]
        y_h = y_h + o_w[h * tq:(h + 1) * tq] * gates[:, 3 * h + 2:3 * h + 3]
        heads.append(y_h)
    y = jnp.concatenate(heads, axis=1)
    y_ref[0] = (y * _silu(zc_ref[0])).astype(BF16)


def _nsa(q, zc, gl, kcv, ksa, vs, kw, vw, overlap):
    bsz, seq, _ = q.shape
    n_cmp = kcv.shape[3]
    gw = NSA_GROUP * HEAD_DIM
    n_sel = min(N_SELECT, seq // SLC_BLOCK)
    per_bg = lambda b, g, i: (b, g, 0, 0)
    return pl.pallas_call(
        functools.partial(_nsa_kernel, n_sel=n_sel),
        grid=(bsz, NSA_KV, seq // TQ),
        in_specs=[
            pl.BlockSpec((1, TQ, gw), lambda b, g, i: (b, i, g)),
            pl.BlockSpec((1, TQ, gw), lambda b, g, i: (b, i, g)),
            pl.BlockSpec((1, TQ, LANES), lambda b, g, i: (b, i, g)),
            pl.BlockSpec((1, 2, 1, n_cmp, HEAD_DIM), lambda b, g, i: (b, 0, g, 0, 0)),
            pl.BlockSpec((1, 1, seq, SEL_LANES + HEAD_DIM), per_bg, pipeline_mode=pl.Buffered(1)),
            pl.BlockSpec((1, 1, seq, 2 * HEAD_DIM), per_bg, pipeline_mode=pl.Buffered(1)),
            pl.BlockSpec((1, 1, seq, HEAD_DIM), per_bg, pipeline_mode=pl.Buffered(1)),
            pl.BlockSpec((1, 1, seq, 2 * HEAD_DIM), per_bg, pipeline_mode=pl.Buffered(1)),
            pl.BlockSpec(overlap.shape, lambda b, g, i: (0, 0), pipeline_mode=pl.Buffered(1)),
        ],
        out_specs=pl.BlockSpec((1, TQ, gw), lambda b, g, i: (b, i, g)),
        out_shape=jax.ShapeDtypeStruct((bsz, seq, NSA_WIDTH), BF16),
        scratch_shapes=[pltpu.VMEM((2, NSA_GROUP * TQ, TK), F32)],
        compiler_params=pltpu.CompilerParams(
            dimension_semantics=("arbitrary", "arbitrary", "arbitrary"), vmem_limit_bytes=VMEM_LIMIT),
        name="nsa",
    )(q, zc, gl, kcv, ksa, vs, kw, vw, overlap)


def _final_kernel(x_ref, part_ref, yc_ref, wout_ref, g_ref, b_ref, o_ref, *, alpha):
    out = part_ref[0] + _dot(yc_ref[0], wout_ref[...])
    y = alpha * x_ref[0] + out
    mu = jnp.mean(y, axis=-1, keepdims=True)
    var = jnp.mean(jnp.square(y - mu), axis=-1, keepdims=True)
    o_ref[0] = ((y - mu) * lax.rsqrt(var + LN_EPS)) * g_ref[...] + b_ref[...]


def _final(x, part, yc, wout_c, ln_g, ln_b, alpha):
    bsz, seq, dm = x.shape
    ts = TS_PROJ
    tok = lambda b, s: (b, s, 0)
    return pl.pallas_call(
        functools.partial(_final_kernel, alpha=alpha),
        grid=(bsz, seq // ts),
        in_specs=[
            pl.BlockSpec((1, ts, dm), tok),
            pl.BlockSpec((1, ts, dm), tok),
            pl.BlockSpec((1, ts, NSA_WIDTH), tok),
            pl.BlockSpec(wout_c.shape, lambda b, s: (0, 0)),
            pl.BlockSpec(ln_g.shape, lambda b, s: (0, 0)),
            pl.BlockSpec(ln_b.shape, lambda b, s: (0, 0)),
        ],
        out_specs=pl.BlockSpec((1, ts, dm), tok),
        out_shape=jax.ShapeDtypeStruct((bsz, seq, dm), F32),
        compiler_params=pltpu.CompilerParams(
            dimension_semantics=("arbitrary", "arbitrary"), vmem_limit_bytes=VMEM_LIMIT),
        name="final",
    )(x, part, yc, wout_c, ln_g, ln_b)


def _pack_w_in(w_in):
    depth, dm, _ = w_in.shape
    gate0 = COL_C + 2 * NSA_WIDTH + 6 * KV_WIDTH
    per_group = N_GATES // NSA_KV
    gates = w_in[:, :, gate0:gate0 + N_GATES]
    gate_blocks = []
    for g in range(NSA_KV):
        blk = gates[:, :, g * per_group:(g + 1) * per_group]
        gate_blocks.append(jnp.pad(blk, ((0, 0), (0, 0), (0, LANES - per_group))))
    d_cols = w_in[:, :, gate0 + N_GATES:]
    return jnp.concatenate([w_in[:, :, :gate0]] + gate_blocks + [d_cols], axis=-1).astype(BF16)


def _block_diag_gates(wa, wx):
    depth = wa.shape[0]
    per_half = LRU_BLOCKS // 2
    half = per_half * LRU_BLOCK
    halves = []
    for hf in range(2):
        mats = []
        for w in (wa, wx):
            m = jnp.zeros((depth, half, half), F32)
            for j in range(per_half):
                m = m.at[:, j * LRU_BLOCK:(j + 1) * LRU_BLOCK, j * LRU_BLOCK:(j + 1) * LRU_BLOCK].set(
                    w[:, hf * per_half + j])
            mats.append(m)
        halves.append(jnp.concatenate(mats, axis=-1))
    return jnp.stack(halves, axis=1).astype(BF16)


def _overlap_matrix(n_cmp_pad, n_cmp, n_slc):
    ci = np.arange(n_cmp_pad)[:, None] * CMP_STRIDE
    sj = np.arange(SEL_LANES)[None, :] * SLC_BLOCK
    ov = (ci < sj + SLC_BLOCK) & (ci + CMP_LEN > sj)
    ov &= (np.arange(n_cmp_pad)[:, None] < n_cmp) & (np.arange(SEL_LANES)[None, :] < n_slc)
    return jnp.asarray(ov.T.astype(np.float32), dtype=BF16)


def kernel(x, mem, w_in, sgu_ln_g, sgu_ln_b, sgu_w, sgu_b, conv_w, conv_b, lru_wa, lru_ba, lru_wx, lru_bx,
           lru_lambda, cmp_pos, cmp_w1, cmp_w2, w_mem_kv, w_out, ln_g, ln_b):
    depth = w_in.shape[0]
    bsz, seq, dm = x.shape
    assert TQ == TK and WINDOW <= TK
    assert seq % TK == 0 and seq // SLC_BLOCK <= SEL_LANES
    alpha = (2 * depth) ** 0.25
    n_chunk = seq // CMP_STRIDE
    n_cmp = (seq - CMP_LEN) // CMP_STRIDE + 1

    w_in_p = _pack_w_in(w_in)
    wbd = _block_diag_gates(lru_wa, lru_wx)
    wmem = w_mem_kv.astype(BF16)
    c0 = A_WIDTH + LRU_WIDTH
    wout_abd = jnp.concatenate([w_out[:, :c0], w_out[:, c0 + NSA_WIDTH:]], axis=1).astype(BF16)
    wout_c = w_out[:, c0:c0 + NSA_WIDTH].astype(BF16)
    w1 = cmp_w1.astype(BF16)
    w2 = cmp_w2.astype(BF16)
    pos_flat = cmp_pos.reshape(depth, 2, 1, CMP_LEN * HEAD_DIM)
    overlap = _overlap_matrix(n_chunk, n_cmp, seq // SLC_BLOCK)
    row = lambda a: a[:, None, :]

    for l in range(depth):
        part, q, zc, gl, kvc_raw, ksa, vs, kw, vw = _proj_mix(
            x, mem, w_in_p[l], row(sgu_ln_g)[l], row(sgu_ln_b)[l], sgu_w[l], sgu_b[l][:, :, None],
            conv_w[l], row(conv_b)[l], wbd[l], row(lru_ba)[l], row(lru_bx)[l], row(lru_lambda)[l],
            wmem[l], wout_abd[l])
        fc = kvc_raw.reshape(bsz, 2, NSA_KV, n_chunk, CMP_STRIDE * HEAD_DIM)
        kcv = _compress(fc, pos_flat[l], w1[l], w2[l])
        yc = _nsa(q, zc, gl, kcv, ksa, vs, kw, vw, overlap)
        x = _final(x, part, yc, wout_c[l], row(ln_g)[l], row(ln_b)[l], alpha)
    return x
```

```python
import functools

import jax
import jax.numpy as jnp
import numpy as np
from jax import lax
from jax.experimental import pallas as pl
from jax.experimental.pallas import tpu as pltpu

F32 = jnp.float32
BF16 = jnp.bfloat16

CHUNK = 128
A_GROUPS = 4
A_WIDTH = 512
LRU_WIDTH = 512
LRU_BLOCKS = 8
LRU_BLOCK = 64
CONV_W = 4
C_LRU = 8.0
HEAD_DIM = 64
NSA_KV = 2
NSA_GROUP = 4
NSA_WIDTH = 512
KV_WIDTH = 128
CMP_LEN = 32
CMP_STRIDE = 16
CMP_HIDDEN = 128
SLC_BLOCK = 64
N_SELECT = 16
WINDOW = 512
N_GATES = 24
MEM_HEADS = 4
MEM_WIDTH = 256
LN_EPS = 1e-5
NEG = -1e30
FORCE_BONUS = 1e4

LANES = 128
SEL_LANES = 128
VMEM_LIMIT = 56 * 1024 * 1024

COL_A = 0
COL_B = 1536
COL_C = 2560
COL_G = 4352
COL_D = 4608
D_IN_P = 5120

TS_PROJ = 512
TQ = 512
TK = 512

Q_SCALE = HEAD_DIM ** -0.5 * float(np.log2(np.e))


def _sigmoid(x):
    return 1.0 / (1.0 + jnp.exp(-x))


def _silu(x):
    return x * _sigmoid(x)


def _gelu(x):
    return jax.nn.gelu(x, approximate=True)


def _dot(a, b):
    return jnp.dot(a, b, preferred_element_type=F32)


def _dot_nt(a, b):
    return lax.dot_general(a, b, (((1,), (1,)), ((), ())), preferred_element_type=F32)


def _proj_mix_kernel(x_ref, mem_ref, w_in_ref, sln_g_ref, sln_b_ref, sw_ref, sb_ref, cw_ref, cb_ref,
                     wbd_ref, ba_ref, bx_ref, lam_ref, wmem_ref, wout_ref,
                     part_ref, q_ref, zc_ref, gl_ref, kvc_ref, ksa_ref, vs_ref, kw_ref, vw_ref,
                     conv_buf, h_carry, memk, memv):
    s = pl.program_id(1)
    ts = x_ref.shape[1]
    xb = x_ref[0].astype(BF16)

    @pl.when(s == 0)
    def _():
        conv_buf[0:8, :] = jnp.zeros((8, LRU_WIDTH), F32)
        h_carry[...] = jnp.zeros_like(h_carry)
        kv = _dot(mem_ref[0].astype(BF16), wmem_ref[...])
        memk[...] = kv[:, :MEM_WIDTH].astype(BF16)
        memv[...] = kv[:, MEM_WIDTH:].astype(BF16)

    pa = _dot(xb, w_in_ref[:, COL_A:COL_A + 3 * A_WIDTH])
    u = _gelu(pa[:, 0:A_WIDTH])
    v = _gelu(pa[:, A_WIDTH:2 * A_WIDTH])
    za = pa[:, 2 * A_WIDTH:3 * A_WIDTH]
    row = lax.broadcasted_iota(jnp.int32, (CHUNK, CHUNK), 0)
    col = lax.broadcasted_iota(jnp.int32, (CHUNK, CHUNK), 1)
    causal = col <= row
    sv_groups = []
    for g in range(A_GROUPS):
        vg = v[:, g * CHUNK:(g + 1) * CHUNK]
        mu = jnp.mean(vg, axis=-1, keepdims=True)
        var = jnp.mean(jnp.square(vg - mu), axis=-1, keepdims=True)
        vn = (vg - mu) * lax.rsqrt(var + LN_EPS)
        vn = vn * sln_g_ref[:, g * CHUNK:(g + 1) * CHUNK] + sln_b_ref[:, g * CHUNK:(g + 1) * CHUNK]
        vnb = vn.astype(BF16)
        wg = jnp.where(causal, sw_ref[g], 0.0).astype(BF16)
        bias = sb_ref[g]
        chunks = []
        for c in range(ts // CHUNK):
            chunks.append(_dot(wg, vnb[c * CHUNK:(c + 1) * CHUNK]) + bias)
        sv_groups.append(jnp.concatenate(chunks, axis=0))
    sv = jnp.concatenate(sv_groups, axis=1)
    ya = (u * sv) * _silu(za)

    pb = _dot(xb, w_in_ref[:, COL_B:COL_B + 2 * LRU_WIDTH])
    xbv = pb[:, :LRU_WIDTH]
    zb = pb[:, LRU_WIDTH:]
    conv_buf[8:8 + ts, :] = xbv
    xc = cb_ref[...] + conv_buf[5:5 + ts, :] * cw_ref[0:1, :]
    xc = xc + conv_buf[6:6 + ts, :] * cw_ref[1:2, :]
    xc = xc + conv_buf[7:7 + ts, :] * cw_ref[2:3, :]
    xc = xc + xbv * cw_ref[3:4, :]
    conv_buf[0:8, :] = xbv[ts - 8:ts, :]
    xcb = xc.astype(BF16)
    half = LRU_WIDTH // 2
    g0 = _dot(xcb[:, :half], wbd_ref[0])
    g1 = _dot(xcb[:, half:], wbd_ref[1])
    r = _sigmoid(jnp.concatenate([g0[:, :half], g1[:, :half]], axis=1) + ba_ref[...])
    ig = _sigmoid(jnp.concatenate([g0[:, half:], g1[:, half:]], axis=1) + bx_ref[...])
    nlam = -lam_ref[...]
    softplus = jnp.maximum(nlam, 0.0) + jnp.log1p(jnp.exp(-jnp.abs(nlam)))
    log_a = (-C_LRU * r) * softplus
    a_cum = jnp.exp(log_a)
    th = jnp.tanh(log_a)
    b_cum = jnp.sqrt((-2.0 * th) / (1.0 - th)) * (ig * xc)
    d = 1
    while d < ts:
        a_sh = jnp.concatenate([jnp.ones((d, LRU_WIDTH), F32), a_cum[:ts - d]], axis=0)
        b_sh = jnp.concatenate([jnp.zeros((d, LRU_WIDTH), F32), b_cum[:ts - d]], axis=0)
        b_cum = b_cum + a_cum * b_sh
        a_cum = a_cum * a_sh
        d *= 2
    h = b_cum + a_cum * h_carry[...]
    h_carry[...] = h[ts - 1:ts, :]
    yb = h * _silu(zb)

    pd = _dot(xb, w_in_ref[:, COL_D:COL_D + 2 * MEM_WIDTH])
    qd = pd[:, :MEM_WIDTH] * (HEAD_DIM ** -0.5)
    zd = pd[:, MEM_WIDTH:]
    head_of_lane = lax.broadcasted_iota(jnp.int32, (1, MEM_WIDTH), 1) // HEAD_DIM
    od = jnp.zeros((ts, MEM_WIDTH), F32)
    for hd in range(MEM_HEADS):
        hm = head_of_lane == hd
        qh = jnp.where(hm, qd, 0.0).astype(BF16)
        sc = _dot_nt(qh, memk[...])
        m = jnp.max(sc, axis=-1, keepdims=True)
        p = jnp.exp(sc - m)
        l = jnp.sum(p, axis=-1, keepdims=True)
        pv = _dot(p.astype(BF16), memv[...])
        od = od + jnp.where(hm, pv * (1.0 / l), 0.0)
    yd = od * _silu(zd)

    y_abd = jnp.concatenate([ya, yb, yd], axis=1).astype(BF16)
    part_ref[0] = _dot(y_abd, wout_ref[...])

    pc = _dot(xb, w_in_ref[:, COL_C:COL_C + 2 * NSA_WIDTH + 6 * KV_WIDTH])
    q_ref[0] = (pc[:, :NSA_WIDTH] * Q_SCALE).astype(BF16)
    zc_ref[0] = pc[:, NSA_WIDTH:2 * NSA_WIDTH]
    gl_ref[0] = _dot(xb, w_in_ref[:, COL_G:COL_G + NSA_KV * LANES])
    base = 2 * NSA_WIDTH
    kpos = s * ts + lax.broadcasted_iota(jnp.int32, (ts, SEL_LANES), 0)
    blk = lax.broadcasted_iota(jnp.int32, (ts, SEL_LANES), 1)
    sel_rows = jnp.where((kpos // SLC_BLOCK) == blk, -NEG, 0.0).astype(BF16)
    ones_col = jnp.where(lax.broadcasted_iota(jnp.int32, (ts, HEAD_DIM), 1) == 0, 1.0, 0.0).astype(BF16)
    for g in range(NSA_KV):
        lo = g * HEAD_DIM
        kvc_ref[0, 0, g] = pc[:, base + lo:base + lo + HEAD_DIM]
        kvc_ref[0, 1, g] = pc[:, base + KV_WIDTH + lo:base + KV_WIDTH + lo + HEAD_DIM]
        ksa_ref[0, g, :, 0:SEL_LANES] = sel_rows
        ksa_ref[0, g, :, SEL_LANES:SEL_LANES + HEAD_DIM] = (
            pc[:, base + 2 * KV_WIDTH + lo:base + 2 * KV_WIDTH + lo + HEAD_DIM].astype(BF16))
        vs_ref[0, g, :, 0:HEAD_DIM] = (
            pc[:, base + 3 * KV_WIDTH + lo:base + 3 * KV_WIDTH + lo + HEAD_DIM].astype(BF16))
        vs_ref[0, g, :, HEAD_DIM:2 * HEAD_DIM] = ones_col
        kw_ref[0, g] = pc[:, base + 4 * KV_WIDTH + lo:base + 4 * KV_WIDTH + lo + HEAD_DIM].astype(BF16)
        vw_ref[0, g, :, 0:HEAD_DIM] = (
            pc[:, base + 5 * KV_WIDTH + lo:base + 5 * KV_WIDTH + lo + HEAD_DIM].astype(BF16))
        vw_ref[0, g, :, HEAD_DIM:2 * HEAD_DIM] = ones_col


def _proj_mix(x, mem, w_in_p, sln_g, sln_b, sw, sb, cw, cb, wbd, ba, bx, lam, wmem, wout_abd):
    bsz, seq, dm = x.shape
    ts = TS_PROJ
    n_mem = mem.shape[1]
    def whole(a):
        return pl.BlockSpec(a.shape, lambda b, s: (0,) * a.ndim, pipeline_mode=pl.Buffered(1))

    in_specs = [
        pl.BlockSpec((1, ts, dm), lambda b, s: (b, s, 0)),
        pl.BlockSpec((1, n_mem, dm), lambda b, s: (b, 0, 0), pipeline_mode=pl.Buffered(1)),
    ] + [whole(a) for a in (w_in_p, sln_g, sln_b, sw, sb, cw, cb, wbd, ba, bx, lam, wmem, wout_abd)]
    kv_spec = pl.BlockSpec((1, NSA_KV, ts, HEAD_DIM), lambda b, s: (b, 0, s, 0))
    va_spec = pl.BlockSpec((1, NSA_KV, ts, 2 * HEAD_DIM), lambda b, s: (b, 0, s, 0))
    out_specs = [
        pl.BlockSpec((1, ts, dm), lambda b, s: (b, s, 0)),
        pl.BlockSpec((1, ts, NSA_WIDTH), lambda b, s: (b, s, 0)),
        pl.BlockSpec((1, ts, NSA_WIDTH), lambda b, s: (b, s, 0)),
        pl.BlockSpec((1, ts, NSA_KV * LANES), lambda b, s: (b, s, 0)),
        pl.BlockSpec((1, 2, NSA_KV, ts, HEAD_DIM), lambda b, s: (b, 0, 0, s, 0)),
        pl.BlockSpec((1, NSA_KV, ts, SEL_LANES + HEAD_DIM), lambda b, s: (b, 0, s, 0)),
        va_spec, kv_spec, va_spec,
    ]
    out_shape = [
        jax.ShapeDtypeStruct((bsz, seq, dm), F32),
        jax.ShapeDtypeStruct((bsz, seq, NSA_WIDTH), BF16),
        jax.ShapeDtypeStruct((bsz, seq, NSA_WIDTH), F32),
        jax.ShapeDtypeStruct((bsz, seq, NSA_KV * LANES), F32),
        jax.ShapeDtypeStruct((bsz, 2, NSA_KV, seq, HEAD_DIM), F32),
        jax.ShapeDtypeStruct((bsz, NSA_KV, seq, SEL_LANES + HEAD_DIM), BF16),
        jax.ShapeDtypeStruct((bsz, NSA_KV, seq, 2 * HEAD_DIM), BF16),
        jax.ShapeDtypeStruct((bsz, NSA_KV, seq, HEAD_DIM), BF16),
        jax.ShapeDtypeStruct((bsz, NSA_KV, seq, 2 * HEAD_DIM), BF16),
    ]
    return pl.pallas_call(
        _proj_mix_kernel,
        grid=(bsz, seq // ts),
        in_specs=in_specs,
        out_specs=out_specs,
        out_shape=out_shape,
        scratch_shapes=[
            pltpu.VMEM((ts + 8, LRU_WIDTH), F32),
            pltpu.VMEM((1, LRU_WIDTH), F32),
            pltpu.VMEM((n_mem, MEM_WIDTH), BF16),
            pltpu.VMEM((n_mem, MEM_WIDTH), BF16),
        ],
        compiler_params=pltpu.CompilerParams(
            dimension_semantics=("arbitrary", "arbitrary"), vmem_limit_bytes=VMEM_LIMIT),
        name="proj_mix",
    )(x, mem, w_in_p, sln_g, sln_b, sw, sb, cw, cb, wbd, ba, bx, lam, wmem, wout_abd)


def _compress_kernel(fc_ref, pos_ref, w1_ref, w2_ref, out_ref):
    fc = fc_ref[0, 0, 0]
    half = (CMP_LEN // 2) * HEAD_DIM
    pos = pos_ref[0]
    fa = (fc + pos[:, :half]).astype(BF16)
    fb = (fc + pos[:, half:]).astype(BF16)
    ha = _dot(fa, w1_ref[0, :half, :])
    hb = _dot(fb, w1_ref[0, half:, :])
    hb_next = jnp.concatenate([hb[1:], hb[:1]], axis=0)
    hid = _gelu(ha + hb_next)
    out_ref[0, 0, 0] = _dot(hid.astype(BF16), w2_ref[0]).astype(BF16)


def _compress(fc, pos_flat, w1, w2):
    bsz, _, _, n_chunk, width = fc.shape
    return pl.pallas_call(
        _compress_kernel,
        grid=(bsz, 2, NSA_KV),
        in_specs=[
            pl.BlockSpec((1, 1, 1, n_chunk, width), lambda b, k, g: (b, k, g, 0, 0)),
            pl.BlockSpec((1, 1, CMP_LEN * HEAD_DIM), lambda b, k, g: (k, 0, 0)),
            pl.BlockSpec((1, CMP_LEN * HEAD_DIM, CMP_HIDDEN), lambda b, k, g: (k, 0, 0)),
            pl.BlockSpec((1, CMP_HIDDEN, HEAD_DIM), lambda b, k, g: (k, 0, 0)),
        ],
        out_specs=pl.BlockSpec((1, 1, 1, n_chunk, HEAD_DIM), lambda b, k, g: (b, k, g, 0, 0)),
        out_shape=jax.ShapeDtypeStruct((bsz, 2, NSA_KV, n_chunk, HEAD_DIM), BF16),
        compiler_params=pltpu.CompilerParams(
            dimension_semantics=("arbitrary", "arbitrary", "arbitrary"), vmem_limit_bytes=VMEM_LIMIT),
        name="compress",
    )(fc, pos_flat, w1, w2)


def _nsa_kernel(q_ref, zc_ref, gl_ref, kcv_ref, ksa_ref, vs_ref, kw_ref, vw_ref, ovt_ref, bias_ref, y_ref, s_buf,
                *, n_sel):
    qi = pl.program_id(2)
    tq = q_ref.shape[1]
    rows = NSA_GROUP * tq
    s0 = qi * tq
    q = q_ref[0]
    q_heads = [q[:, h * HEAD_DIM:(h + 1) * HEAD_DIM] for h in range(NSA_GROUP)]
    qs = jnp.concatenate(q_heads, axis=0)
    t_q = s0 + lax.broadcasted_iota(jnp.int32, (tq, 1), 0)

    kc = kcv_ref[0, 0, 0]
    vc = kcv_ref[0, 1, 0]
    n_cmp = kc.shape[0]
    cmp_end = lax.broadcasted_iota(jnp.int32, (1, n_cmp), 1) * CMP_STRIDE + (CMP_LEN - 1)
    valid_c = cmp_end <= t_q
    any_valid = t_q >= CMP_LEN - 1
    o_c = []
    ps = None
    for h in range(NSA_GROUP):
        sc = jnp.where(valid_c, _dot_nt(q_heads[h], kc), NEG)
        m = jnp.max(sc, axis=-1, keepdims=True)
        p = jnp.exp2(sc - m)
        l = jnp.sum(p, axis=-1, keepdims=True)
        pn = p * jnp.where(any_valid, 1.0 / l, 0.0)
        o_c.append(_dot(pn.astype(BF16), vc))
        ps = pn if ps is None else ps + pn

    n_tiles = ksa_ref.shape[2] // TK

    def tile_start(kt):
        return pl.multiple_of(jnp.clip(kt, 0, n_tiles - 1) * TK, TK)

    one = 1 + jnp.minimum(qi, 0)

    def run_tiles(q_op, k_ref, v_ref, kt0, stride, n_plain, n_masked, bias_fn, side_fn=None, side=None):
        def tile(ref, i):
            return ref[0, 0, pl.ds(tile_start(kt0 + stride * i), TK), :]

        def step(i, slot, carry, prefetch, masked, pos=0):
            (m_i, acc), side_i = carry
            s = s_buf[slot]
            if prefetch:
                s_buf[1 - slot] = _dot_nt(q_op, tile(k_ref, i + 1))
            if masked:
                s = (s.reshape(NSA_GROUP, tq, TK) + bias_fn(pos)[None]).reshape(rows, TK)
            m_new = jnp.maximum(m_i, jnp.max(s, axis=-1, keepdims=True))
            p_t = jnp.exp2(s - m_new).astype(BF16)
            acc = jnp.exp2(m_i - m_new) * acc + _dot(p_t, tile(v_ref, i))
            return (m_new, acc), (side_fn(side_i) if side_fn is not None else side_i)

        def group(width, first, masked, prefetch_last=True):
            def body(j, carry):
                for i in range(width):
                    carry = step(first + width * j + i, i % 2, carry, prefetch_last or i + 1 < width, masked, i)
                return carry
            return body

        s_buf[0] = _dot_nt(q_op, tile(k_ref, 0))
        carry = ((jnp.full((rows, 1), NEG, F32), jnp.zeros((rows, 2 * HEAD_DIM), F32)), side)
        if isinstance(n_plain, int):
            assert n_plain == 0 and n_masked == 2
            carry = lax.fori_loop(0, one, group(2, 0, True, prefetch_last=False), carry)
        else:
            assert n_masked == 1
            n4 = n_plain // 4
            carry = lax.fori_loop(0, n4, group(4, 0, False), carry)
            carry = lax.fori_loop(0, (n_plain % 4) // 2, group(2, 4 * n4, False), carry)
            carry = lax.fori_loop(0, n_plain % 2, group(1, n_plain - 1, False), carry)
            carry = lax.fori_loop(0, one, lambda _, c: step(n_plain, n_plain % 2, c, False, True), carry)
        acc = carry[0][1]
        return acc[:, :HEAD_DIM] * (1.0 / acc[:, HEAD_DIM:HEAD_DIM + 1]), carry[1]

    ps_hi = ps.astype(BF16)
    ps_lo = (ps - ps_hi.astype(F32)).astype(BF16)
    imp = _dot_nt(ovt_ref[...], ps_hi) + _dot_nt(ovt_ref[...], ps_lo)
    cur = (s0 + lax.broadcasted_iota(jnp.int32, (1, tq), 1)) // SLC_BLOCK
    blk = lax.broadcasted_iota(jnp.int32, (SEL_LANES, 1), 0)
    future = blk > cur
    forced = (blk == 0) | (blk == cur) | (blk == cur - 1)
    work = jnp.where(future, -1.0, imp + jnp.where(forced, FORCE_BONUS, 0.0))
    blk_f = blk.astype(F32)

    def knock_out(work_i, rounds):
        for _ in range(rounds):
            mx = jnp.max(work_i, axis=0, keepdims=True)
            idx = jnp.min(jnp.where(work_i == mx, blk_f, float(SEL_LANES)), axis=0, keepdims=True)
            work_i = jnp.where(blk_f == idx, -jnp.inf, work_i)
        return work_i

    assert tq == TK == WINDOW and n_sel % 2 == 0
    older = jnp.where(qi > 0, bias_ref[1], NEG)
    o_w, work = run_tiles(qs, kw_ref, vw_ref, qi, -1, 0, 2, lambda i: bias_ref[0] if i == 0 else older,
                          side_fn=lambda w: knock_out(w, n_sel // 2), side=work)
    sel_t = jnp.where(work == -jnp.inf, 0.0, -1.0)
    sel_rows = jnp.concatenate([sel_t.T.astype(BF16)] * NSA_GROUP, axis=0)
    q_aug = jnp.concatenate([sel_rows, qs], axis=1)

    o_s, _ = run_tiles(q_aug, ksa_ref, vs_ref, 0, 1, qi, 1, lambda i: bias_ref[0])

    gates = _sigmoid(gl_ref[0])
    heads = []
    for h in range(NSA_GROUP):
        y_h = o_c[h] * gates[:, 3 * h:3 * h + 1]
        y_h = y_h + o_s[h * tq:(h + 1) * tq] * gates[:, 3 * h + 1:3 * h + 2]
        y_h = y_h + o_w[h * tq:(h + 1) * tq] * gates[:, 3 * h + 2:3 * h + 3]
        heads.append(y_h)
    y = jnp.concatenate(heads, axis=1)
    y_ref[0] = (y * _silu(zc_ref[0])).astype(BF16)


def _nsa(q, zc, gl, kcv, ksa, vs, kw, vw, overlap, tile_bias):
    bsz, seq, _ = q.shape
    n_cmp = kcv.shape[3]
    gw = NSA_GROUP * HEAD_DIM
    n_sel = min(N_SELECT, seq // SLC_BLOCK)
    per_bg = lambda b, g, i: (b, g, 0, 0)
    return pl.pallas_call(
        functools.partial(_nsa_kernel, n_sel=n_sel),
        grid=(bsz, NSA_KV, seq // TQ),
        in_specs=[
            pl.BlockSpec((1, TQ, gw), lambda b, g, i: (b, i, g)),
            pl.BlockSpec((1, TQ, gw), lambda b, g, i: (b, i, g)),
            pl.BlockSpec((1, TQ, LANES), lambda b, g, i: (b, i, g)),
            pl.BlockSpec((1, 2, 1, n_cmp, HEAD_DIM), lambda b, g, i: (b, 0, g, 0, 0)),
            pl.BlockSpec((1, 1, seq, SEL_LANES + HEAD_DIM), per_bg, pipeline_mode=pl.Buffered(1)),
            pl.BlockSpec((1, 1, seq, 2 * HEAD_DIM), per_bg, pipeline_mode=pl.Buffered(1)),
            pl.BlockSpec((1, 1, seq, HEAD_DIM), per_bg, pipeline_mode=pl.Buffered(1)),
            pl.BlockSpec((1, 1, seq, 2 * HEAD_DIM), per_bg, pipeline_mode=pl.Buffered(1)),
            pl.BlockSpec(overlap.shape, lambda b, g, i: (0, 0), pipeline_mode=pl.Buffered(1)),
            pl.BlockSpec(tile_bias.shape, lambda b, g, i: (0, 0, 0), pipeline_mode=pl.Buffered(1)),
        ],
        out_specs=pl.BlockSpec((1, TQ, gw), lambda b, g, i: (b, i, g)),
        out_shape=jax.ShapeDtypeStruct((bsz, seq, NSA_WIDTH), BF16),
        scratch_shapes=[pltpu.VMEM((2, NSA_GROUP * TQ, TK), F32)],
        compiler_params=pltpu.CompilerParams(
            dimension_semantics=("arbitrary", "arbitrary", "arbitrary"), vmem_limit_bytes=VMEM_LIMIT),
        name="nsa",
    )(q, zc, gl, kcv, ksa, vs, kw, vw, overlap, tile_bias)


def _final_kernel(x_ref, part_ref, yc_ref, wout_ref, g_ref, b_ref, o_ref, *, alpha):
    out = part_ref[0] + _dot(yc_ref[0], wout_ref[...])
    y = alpha * x_ref[0] + out
    mu = jnp.mean(y, axis=-1, keepdims=True)
    var = jnp.mean(jnp.square(y - mu), axis=-1, keepdims=True)
    o_ref[0] = ((y - mu) * lax.rsqrt(var + LN_EPS)) * g_ref[...] + b_ref[...]


def _final(x, part, yc, wout_c, ln_g, ln_b, alpha):
    bsz, seq, dm = x.shape
    ts = TS_PROJ
    tok = lambda b, s: (b, s, 0)
    return pl.pallas_call(
        functools.partial(_final_kernel, alpha=alpha),
        grid=(bsz, seq // ts),
        in_specs=[
            pl.BlockSpec((1, ts, dm), tok),
            pl.BlockSpec((1, ts, dm), tok),
            pl.BlockSpec((1, ts, NSA_WIDTH), tok),
            pl.BlockSpec(wout_c.shape, lambda b, s: (0, 0)),
            pl.BlockSpec(ln_g.shape, lambda b, s: (0, 0)),
            pl.BlockSpec(ln_b.shape, lambda b, s: (0, 0)),
        ],
        out_specs=pl.BlockSpec((1, ts, dm), tok),
        out_shape=jax.ShapeDtypeStruct((bsz, seq, dm), F32),
        compiler_params=pltpu.CompilerParams(
            dimension_semantics=("arbitrary", "arbitrary"), vmem_limit_bytes=VMEM_LIMIT),
        name="final",
    )(x, part, yc, wout_c, ln_g, ln_b)


def _pack_w_in(w_in):
    depth, dm, _ = w_in.shape
    gate0 = COL_C + 2 * NSA_WIDTH + 6 * KV_WIDTH
    per_group = N_GATES // NSA_KV
    gates = w_in[:, :, gate0:gate0 + N_GATES]
    gate_blocks = []
    for g in range(NSA_KV):
        blk = gates[:, :, g * per_group:(g + 1) * per_group]
        gate_blocks.append(jnp.pad(blk, ((0, 0), (0, 0), (0, LANES - per_group))))
    d_cols = w_in[:, :, gate0 + N_GATES:]
    return jnp.concatenate([w_in[:, :, :gate0]] + gate_blocks + [d_cols], axis=-1).astype(BF16)


def _block_diag_gates(wa, wx):
    depth = wa.shape[0]
    per_half = LRU_BLOCKS // 2
    half = per_half * LRU_BLOCK
    halves = []
    for hf in range(2):
        mats = []
        for w in (wa, wx):
            m = jnp.zeros((depth, half, half), F32)
            for j in range(per_half):
                m = m.at[:, j * LRU_BLOCK:(j + 1) * LRU_BLOCK, j * LRU_BLOCK:(j + 1) * LRU_BLOCK].set(
                    w[:, hf * per_half + j])
            mats.append(m)
        halves.append(jnp.concatenate(mats, axis=-1))
    return jnp.stack(halves, axis=1).astype(BF16)


def _overlap_matrix(n_cmp_pad, n_cmp, n_slc):
    ci = np.arange(n_cmp_pad)[:, None] * CMP_STRIDE
    sj = np.arange(SEL_LANES)[None, :] * SLC_BLOCK
    ov = (ci < sj + SLC_BLOCK) & (ci + CMP_LEN > sj)
    ov &= (np.arange(n_cmp_pad)[:, None] < n_cmp) & (np.arange(SEL_LANES)[None, :] < n_slc)
    return jnp.asarray(ov.T.astype(np.float32), dtype=BF16)


def _tile_bias():
    r = np.arange(TQ)[:, None]
    k = np.arange(TK)[None, :]
    diag = np.where(k <= r, 0.0, NEG)
    older = np.where(k > r, 0.0, NEG)
    return jnp.asarray(np.stack([diag, older]), dtype=F32)


def kernel(x, mem, w_in, sgu_ln_g, sgu_ln_b, sgu_w, sgu_b, conv_w, conv_b, lru_wa, lru_ba, lru_wx, lru_bx,
           lru_lambda, cmp_pos, cmp_w1, cmp_w2, w_mem_kv, w_out, ln_g, ln_b):
    depth = w_in.shape[0]
    bsz, seq, dm = x.shape
    assert TQ == TK and WINDOW <= TK
    assert seq % TK == 0 and seq // SLC_BLOCK <= SEL_LANES
    alpha = (2 * depth) ** 0.25
    n_chunk = seq // CMP_STRIDE
    n_cmp = (seq - CMP_LEN) // CMP_STRIDE + 1

    w_in_p = _pack_w_in(w_in)
    wbd = _block_diag_gates(lru_wa, lru_wx)
    wmem = w_mem_kv.astype(BF16)
    c0 = A_WIDTH + LRU_WIDTH
    wout_abd = jnp.concatenate([w_out[:, :c0], w_out[:, c0 + NSA_WIDTH:]], axis=1).astype(BF16)
    wout_c = w_out[:, c0:c0 + NSA_WIDTH].astype(BF16)
    w1 = cmp_w1.astype(BF16)
    w2 = cmp_w2.astype(BF16)
    pos_flat = cmp_pos.reshape(depth, 2, 1, CMP_LEN * HEAD_DIM)
    overlap = _overlap_matrix(n_chunk, n_cmp, seq // SLC_BLOCK)
    tile_bias = _tile_bias()
    row = lambda a: a[:, None, :]

    for l in range(depth):
        part, q, zc, gl, kvc_raw, ksa, vs, kw, vw = _proj_mix(
            x, mem, w_in_p[l], row(sgu_ln_g)[l], row(sgu_ln_b)[l], sgu_w[l], sgu_b[l][:, :, None],
            conv_w[l], row(conv_b)[l], wbd[l], row(lru_ba)[l], row(lru_bx)[l], row(lru_lambda)[l],
            wmem[l], wout_abd[l])
        fc = kvc_raw.reshape(bsz, 2, NSA_KV, n_chunk, CMP_STRIDE * HEAD_DIM)
        kcv = _compress(fc, pos_flat[l], w1[l], w2[l])
        yc = _nsa(q, zc, gl, kcv, ksa, vs, kw, vw, overlap, tile_bias)
        x = _final(x, part, yc, wout_c[l], row(ln_g)[l], row(ln_b)[l], alpha)
    return x
```

```python
import functools

import jax
import jax.numpy as jnp
import numpy as np
from jax import lax
from jax.experimental import pallas as pl
from jax.experimental.pallas import tpu as pltpu

F32 = jnp.float32
BF16 = jnp.bfloat16

CHUNK = 128
A_GROUPS = 4
A_WIDTH = 512
LRU_WIDTH = 512
LRU_BLOCKS = 8
LRU_BLOCK = 64
CONV_W = 4
C_LRU = 8.0
HEAD_DIM = 64
NSA_KV = 2
NSA_GROUP = 4
NSA_WIDTH = 512
KV_WIDTH = 128
CMP_LEN = 32
CMP_STRIDE = 16
CMP_HIDDEN = 128
SLC_BLOCK = 64
N_SELECT = 16
WINDOW = 512
N_GATES = 24
MEM_HEADS = 4
MEM_WIDTH = 256
LN_EPS = 1e-5
NEG = -1e30
FORCE_BONUS = 1e4

LANES = 128
SEL_LANES = 128
VMEM_LIMIT = 56 * 1024 * 1024

COL_A = 0
COL_B = 1536
COL_C = 2560
COL_G = 4352
COL_D = 4608
D_IN_P = 5120

TS_PROJ = 512
TQ = 512
TK = 512

Q_SCALE = HEAD_DIM ** -0.5 * float(np.log2(np.e))


def _sigmoid(x):
    return 1.0 / (1.0 + jnp.exp(-x))


def _silu(x):
    return x * _sigmoid(x)


def _gelu(x):
    return jax.nn.gelu(x, approximate=True)


def _dot(a, b):
    return jnp.dot(a, b, preferred_element_type=F32)


def _dot_nt(a, b):
    return lax.dot_general(a, b, (((1,), (1,)), ((), ())), preferred_element_type=F32)


def _layer_output(x, part, yc, wout_c, g, b, alpha):
    y = alpha * x + (part + _dot(yc, wout_c))
    mu = jnp.mean(y, axis=-1, keepdims=True)
    var = jnp.mean(jnp.square(y - mu), axis=-1, keepdims=True)
    return ((y - mu) * lax.rsqrt(var + LN_EPS)) * g + b


def _proj_mix_kernel(*refs, alpha):
    if alpha is not None:
        x_ref, part_p_ref, yc_p_ref, woutc_p_ref, lng_p_ref, lnb_p_ref = refs[:6]
        refs = refs[6:]
    else:
        x_ref, refs = refs[0], refs[1:]
    (mem_ref, w_in_ref, sln_g_ref, sln_b_ref, sw_ref, sb_ref, cw_ref, cb_ref, wbd_ref, ba_ref, bx_ref, lam_ref,
     wmem_ref, wout_ref) = refs[:14]
    refs = refs[14:]
    if alpha is not None:
        xo_ref, refs = refs[0], refs[1:]
    (part_ref, q_ref, zc_ref, gl_ref, kvc_ref, ksa_ref, vs_ref, kw_ref, vw_ref,
     conv_buf, h_carry, memk, memv) = refs
    s = pl.program_id(1)
    ts = x_ref.shape[1]
    if alpha is not None:
        x_new = _layer_output(x_ref[0], part_p_ref[0], yc_p_ref[0], woutc_p_ref[...], lng_p_ref[...],
                              lnb_p_ref[...], alpha)
        xo_ref[0] = x_new
        xb = x_new.astype(BF16)
    else:
        xb = x_ref[0].astype(BF16)

    @pl.when(s == 0)
    def _():
        conv_buf[0:8, :] = jnp.zeros((8, LRU_WIDTH), F32)
        h_carry[...] = jnp.zeros_like(h_carry)
        kv = _dot(mem_ref[0].astype(BF16), wmem_ref[...])
        memk[...] = kv[:, :MEM_WIDTH].astype(BF16)
        memv[...] = kv[:, MEM_WIDTH:].astype(BF16)

    pa = _dot(xb, w_in_ref[:, COL_A:COL_A + 3 * A_WIDTH])
    u = _gelu(pa[:, 0:A_WIDTH])
    v = _gelu(pa[:, A_WIDTH:2 * A_WIDTH])
    za = pa[:, 2 * A_WIDTH:3 * A_WIDTH]
    row = lax.broadcasted_iota(jnp.int32, (CHUNK, CHUNK), 0)
    col = lax.broadcasted_iota(jnp.int32, (CHUNK, CHUNK), 1)
    causal = col <= row
    sv_groups = []
    for g in range(A_GROUPS):
        vg = v[:, g * CHUNK:(g + 1) * CHUNK]
        mu = jnp.mean(vg, axis=-1, keepdims=True)
        var = jnp.mean(jnp.square(vg - mu), axis=-1, keepdims=True)
        vn = (vg - mu) * lax.rsqrt(var + LN_EPS)
        vn = vn * sln_g_ref[:, g * CHUNK:(g + 1) * CHUNK] + sln_b_ref[:, g * CHUNK:(g + 1) * CHUNK]
        vnb = vn.astype(BF16)
        wg = jnp.where(causal, sw_ref[g], 0.0).astype(BF16)
        bias = sb_ref[g]
        chunks = []
        for c in range(ts // CHUNK):
            chunks.append(_dot(wg, vnb[c * CHUNK:(c + 1) * CHUNK]) + bias)
        sv_groups.append(jnp.concatenate(chunks, axis=0))
    sv = jnp.concatenate(sv_groups, axis=1)
    ya = (u * sv) * _silu(za)

    pb = _dot(xb, w_in_ref[:, COL_B:COL_B + 2 * LRU_WIDTH])
    xbv = pb[:, :LRU_WIDTH]
    zb = pb[:, LRU_WIDTH:]
    conv_buf[8:8 + ts, :] = xbv
    xc = cb_ref[...] + conv_buf[5:5 + ts, :] * cw_ref[0:1, :]
    xc = xc + conv_buf[6:6 + ts, :] * cw_ref[1:2, :]
    xc = xc + conv_buf[7:7 + ts, :] * cw_ref[2:3, :]
    xc = xc + xbv * cw_ref[3:4, :]
    conv_buf[0:8, :] = xbv[ts - 8:ts, :]
    xcb = xc.astype(BF16)
    half = LRU_WIDTH // 2
    g0 = _dot(xcb[:, :half], wbd_ref[0])
    g1 = _dot(xcb[:, half:], wbd_ref[1])
    r = _sigmoid(jnp.concatenate([g0[:, :half], g1[:, :half]], axis=1) + ba_ref[...])
    ig = _sigmoid(jnp.concatenate([g0[:, half:], g1[:, half:]], axis=1) + bx_ref[...])
    nlam = -lam_ref[...]
    softplus = jnp.maximum(nlam, 0.0) + jnp.log1p(jnp.exp(-jnp.abs(nlam)))
    log_a = (-C_LRU * r) * softplus
    a_cum = jnp.exp(log_a)
    th = jnp.tanh(log_a)
    b_cum = jnp.sqrt((-2.0 * th) / (1.0 - th)) * (ig * xc)
    d = 1
    while d < ts:
        a_sh = jnp.concatenate([jnp.ones((d, LRU_WIDTH), F32), a_cum[:ts - d]], axis=0)
        b_sh = jnp.concatenate([jnp.zeros((d, LRU_WIDTH), F32), b_cum[:ts - d]], axis=0)
        b_cum = b_cum + a_cum * b_sh
        a_cum = a_cum * a_sh
        d *= 2
    h = b_cum + a_cum * h_carry[...]
    h_carry[...] = h[ts - 1:ts, :]
    yb = h * _silu(zb)

    pd = _dot(xb, w_in_ref[:, COL_D:COL_D + 2 * MEM_WIDTH])
    qd = pd[:, :MEM_WIDTH] * (HEAD_DIM ** -0.5)
    zd = pd[:, MEM_WIDTH:]
    head_of_lane = lax.broadcasted_iota(jnp.int32, (1, MEM_WIDTH), 1) // HEAD_DIM
    od = jnp.zeros((ts, MEM_WIDTH), F32)
    for hd in range(MEM_HEADS):
        hm = head_of_lane == hd
        qh = jnp.where(hm, qd, 0.0).astype(BF16)
        sc = _dot_nt(qh, memk[...])
        m = jnp.max(sc, axis=-1, keepdims=True)
        p = jnp.exp(sc - m)
        l = jnp.sum(p, axis=-1, keepdims=True)
        pv = _dot(p.astype(BF16), memv[...])
        od = od + jnp.where(hm, pv * (1.0 / l), 0.0)
    yd = od * _silu(zd)

    y_abd = jnp.concatenate([ya, yb, yd], axis=1).astype(BF16)
    part_ref[0] = _dot(y_abd, wout_ref[...])

    pc = _dot(xb, w_in_ref[:, COL_C:COL_C + 2 * NSA_WIDTH + 6 * KV_WIDTH])
    q_ref[0] = (pc[:, :NSA_WIDTH] * Q_SCALE).astype(BF16)
    zc_ref[0] = pc[:, NSA_WIDTH:2 * NSA_WIDTH]
    gl_ref[0] = _dot(xb, w_in_ref[:, COL_G:COL_G + NSA_KV * LANES])
    base = 2 * NSA_WIDTH
    kpos = s * ts + lax.broadcasted_iota(jnp.int32, (ts, SEL_LANES), 0)
    blk = lax.broadcasted_iota(jnp.int32, (ts, SEL_LANES), 1)
    sel_rows = jnp.where((kpos // SLC_BLOCK) == blk, -NEG, 0.0).astype(BF16)
    ones_col = jnp.where(lax.broadcasted_iota(jnp.int32, (ts, HEAD_DIM), 1) == 0, 1.0, 0.0).astype(BF16)
    for g in range(NSA_KV):
        lo = g * HEAD_DIM
        kvc_ref[0, 0, g] = pc[:, base + lo:base + lo + HEAD_DIM].astype(BF16)
        kvc_ref[0, 1, g] = pc[:, base + KV_WIDTH + lo:base + KV_WIDTH + lo + HEAD_DIM].astype(BF16)
        ksa_ref[0, g, :, 0:SEL_LANES] = sel_rows
        ksa_ref[0, g, :, SEL_LANES:SEL_LANES + HEAD_DIM] = (
            pc[:, base + 2 * KV_WIDTH + lo:base + 2 * KV_WIDTH + lo + HEAD_DIM].astype(BF16))
        vs_ref[0, g, :, 0:HEAD_DIM] = (
            pc[:, base + 3 * KV_WIDTH + lo:base + 3 * KV_WIDTH + lo + HEAD_DIM].astype(BF16))
        vs_ref[0, g, :, HEAD_DIM:2 * HEAD_DIM] = ones_col
        kw_ref[0, g] = pc[:, base + 4 * KV_WIDTH + lo:base + 4 * KV_WIDTH + lo + HEAD_DIM].astype(BF16)
        vw_ref[0, g, :, 0:HEAD_DIM] = (
            pc[:, base + 5 * KV_WIDTH + lo:base + 5 * KV_WIDTH + lo + HEAD_DIM].astype(BF16))
        vw_ref[0, g, :, HEAD_DIM:2 * HEAD_DIM] = ones_col


def _proj_mix(x, prev, alpha, mem, w_in_p, sln_g, sln_b, sw, sb, cw, cb, wbd, ba, bx, lam, wmem, wout_abd):
    bsz, seq, dm = x.shape
    ts = TS_PROJ
    n_mem = mem.shape[1]
    tok = lambda b, s: (b, s, 0)

    def whole(a):
        return pl.BlockSpec(a.shape, lambda b, s: (0,) * a.ndim, pipeline_mode=pl.Buffered(1))

    consts = (w_in_p, sln_g, sln_b, sw, sb, cw, cb, wbd, ba, bx, lam, wmem, wout_abd)
    args = [x]
    in_specs = [pl.BlockSpec((1, ts, dm), tok)]
    if prev is not None:
        part_p, yc_p, wout_c_p, ln_g_p, ln_b_p = prev
        args += [part_p, yc_p, wout_c_p, ln_g_p, ln_b_p]
        in_specs += [pl.BlockSpec((1, ts, dm), tok), pl.BlockSpec((1, ts, NSA_WIDTH), tok),
                     whole(wout_c_p), whole(ln_g_p), whole(ln_b_p)]
    args += [mem, *consts]
    in_specs += [pl.BlockSpec((1, n_mem, dm), lambda b, s: (b, 0, 0), pipeline_mode=pl.Buffered(1))]
    in_specs += [whole(a) for a in consts]
    kv_spec = pl.BlockSpec((1, NSA_KV, ts, HEAD_DIM), lambda b, s: (b, 0, s, 0))
    va_spec = pl.BlockSpec((1, NSA_KV, ts, 2 * HEAD_DIM), lambda b, s: (b, 0, s, 0))
    out_specs = [
        pl.BlockSpec((1, ts, dm), lambda b, s: (b, s, 0)),
        pl.BlockSpec((1, ts, NSA_WIDTH), lambda b, s: (b, s, 0)),
        pl.BlockSpec((1, ts, NSA_WIDTH), lambda b, s: (b, s, 0)),
        pl.BlockSpec((1, ts, NSA_KV * LANES), lambda b, s: (b, s, 0)),
        pl.BlockSpec((1, 2, NSA_KV, ts, HEAD_DIM), lambda b, s: (b, 0, 0, s, 0)),
        pl.BlockSpec((1, NSA_KV, ts, SEL_LANES + HEAD_DIM), lambda b, s: (b, 0, s, 0)),
        va_spec, kv_spec, va_spec,
    ]
    out_shape = [
        jax.ShapeDtypeStruct((bsz, seq, dm), F32),
        jax.ShapeDtypeStruct((bsz, seq, NSA_WIDTH), BF16),
        jax.ShapeDtypeStruct((bsz, seq, NSA_WIDTH), F32),
        jax.ShapeDtypeStruct((bsz, seq, NSA_KV * LANES), F32),
        jax.ShapeDtypeStruct((bsz, 2, NSA_KV, seq, HEAD_DIM), BF16),
        jax.ShapeDtypeStruct((bsz, NSA_KV, seq, SEL_LANES + HEAD_DIM), BF16),
        jax.ShapeDtypeStruct((bsz, NSA_KV, seq, 2 * HEAD_DIM), BF16),
        jax.ShapeDtypeStruct((bsz, NSA_KV, seq, HEAD_DIM), BF16),
        jax.ShapeDtypeStruct((bsz, NSA_KV, seq, 2 * HEAD_DIM), BF16),
    ]
    if prev is not None:
        out_specs = [pl.BlockSpec((1, ts, dm), tok)] + out_specs
        out_shape = [jax.ShapeDtypeStruct((bsz, seq, dm), F32)] + out_shape
    return pl.pallas_call(
        functools.partial(_proj_mix_kernel, alpha=alpha if prev is not None else None),
        grid=(bsz, seq // ts),
        in_specs=in_specs,
        out_specs=out_specs,
        out_shape=out_shape,
        scratch_shapes=[
            pltpu.VMEM((ts + 8, LRU_WIDTH), F32),
            pltpu.VMEM((1, LRU_WIDTH), F32),
            pltpu.VMEM((n_mem, MEM_WIDTH), BF16),
            pltpu.VMEM((n_mem, MEM_WIDTH), BF16),
        ],
        compiler_params=pltpu.CompilerParams(
            dimension_semantics=("arbitrary", "arbitrary"), vmem_limit_bytes=VMEM_LIMIT),
        name="proj_mix",
    )(*args)


def _compress_kernel(fc_ref, pos_ref, w1_ref, w2_ref, out_ref):
    fc = fc_ref[0, 0, 0].astype(F32)
    half = (CMP_LEN // 2) * HEAD_DIM
    pos = pos_ref[0]
    fa = (fc + pos[:, :half]).astype(BF16)
    fb = (fc + pos[:, half:]).astype(BF16)
    ha = _dot(fa, w1_ref[0, :half, :])
    hb = _dot(fb, w1_ref[0, half:, :])
    hb_next = jnp.concatenate([hb[1:], hb[:1]], axis=0)
    hid = _gelu(ha + hb_next)
    out_ref[0, 0, 0] = _dot(hid.astype(BF16), w2_ref[0]).astype(BF16)


def _compress(fc, pos_flat, w1, w2):
    bsz, _, _, n_chunk, width = fc.shape
    return pl.pallas_call(
        _compress_kernel,
        grid=(bsz, 2, NSA_KV),
        in_specs=[
            pl.BlockSpec((1, 1, 1, n_chunk, width), lambda b, k, g: (b, k, g, 0, 0)),
            pl.BlockSpec((1, 1, CMP_LEN * HEAD_DIM), lambda b, k, g: (k, 0, 0)),
            pl.BlockSpec((1, CMP_LEN * HEAD_DIM, CMP_HIDDEN), lambda b, k, g: (k, 0, 0)),
            pl.BlockSpec((1, CMP_HIDDEN, HEAD_DIM), lambda b, k, g: (k, 0, 0)),
        ],
        out_specs=pl.BlockSpec((1, 1, 1, n_chunk, HEAD_DIM), lambda b, k, g: (b, k, g, 0, 0)),
        out_shape=jax.ShapeDtypeStruct((bsz, 2, NSA_KV, n_chunk, HEAD_DIM), BF16),
        compiler_params=pltpu.CompilerParams(
            dimension_semantics=("arbitrary", "arbitrary", "arbitrary"), vmem_limit_bytes=VMEM_LIMIT),
        name="compress",
    )(fc, pos_flat, w1, w2)


def _nsa_kernel(q_ref, zc_ref, gl_ref, kcv_ref, ksa_ref, vs_ref, kw_ref, vw_ref, ovt_ref, bias_ref, y_ref, s_buf,
                *, n_sel):
    qi = pl.program_id(2)
    tq = q_ref.shape[1]
    rows = NSA_GROUP * tq
    s0 = qi * tq
    q = q_ref[0]
    q_heads = [q[:, h * HEAD_DIM:(h + 1) * HEAD_DIM] for h in range(NSA_GROUP)]
    qs = jnp.concatenate(q_heads, axis=0)
    t_q = s0 + lax.broadcasted_iota(jnp.int32, (tq, 1), 0)

    blk = lax.broadcasted_iota(jnp.int32, (SEL_LANES, 1), 0)
    blk_f = blk.astype(F32)

    def compressed_branch():
        kc = kcv_ref[0, 0, 0]
        vc = kcv_ref[0, 1, 0]
        n_cmp = kc.shape[0]
        cmp_end = lax.broadcasted_iota(jnp.int32, (1, n_cmp), 1) * CMP_STRIDE + (CMP_LEN - 1)
        valid_c = cmp_end <= t_q
        any_valid = t_q >= CMP_LEN - 1
        o_c = []
        ps = None
        for h in range(NSA_GROUP):
            sc = jnp.where(valid_c, _dot_nt(q_heads[h], kc), NEG)
            m = jnp.max(sc, axis=-1, keepdims=True)
            p = jnp.exp2(sc - m)
            l = jnp.sum(p, axis=-1, keepdims=True)
            pn = p * jnp.where(any_valid, 1.0 / l, 0.0)
            o_c.append(_dot(pn.astype(BF16), vc))
            ps = pn if ps is None else ps + pn
        ps_hi = ps.astype(BF16)
        ps_lo = (ps - ps_hi.astype(F32)).astype(BF16)
        imp = _dot_nt(ovt_ref[...], ps_hi) + _dot_nt(ovt_ref[...], ps_lo)
        cur = (s0 + lax.broadcasted_iota(jnp.int32, (1, tq), 1)) // SLC_BLOCK
        future = blk > cur
        forced = (blk == 0) | (blk == cur) | (blk == cur - 1)
        return jnp.where(future, -1.0, imp + jnp.where(forced, FORCE_BONUS, 0.0)), tuple(o_c)

    def knock_out(work_i, rounds):
        for _ in range(rounds):
            mx = jnp.max(work_i, axis=0, keepdims=True)
            idx = jnp.min(jnp.where(work_i == mx, blk_f, float(SEL_LANES)), axis=0, keepdims=True)
            work_i = jnp.where(blk_f == idx, -jnp.inf, work_i)
        return work_i

    n_tiles = ksa_ref.shape[2] // TK

    def tile_start(kt):
        return pl.multiple_of(jnp.clip(kt, 0, n_tiles - 1) * TK, TK)

    one = 1 + jnp.minimum(qi, 0)

    def run_tiles(q_op, k_ref, v_ref, kt0, stride, n_plain, n_masked, bias_fn, side_fn=None, side=None):
        def tile(ref, i):
            return ref[0, 0, pl.ds(tile_start(kt0 + stride * i), TK), :]

        def step(i, slot, carry, prefetch, masked, pos=0):
            (m_i, acc), side_i = carry
            s = s_buf[slot]
            if prefetch:
                s_buf[1 - slot] = _dot_nt(q_op, tile(k_ref, i + 1))
            if masked:
                s = (s.reshape(NSA_GROUP, tq, TK) + bias_fn(pos)[None]).reshape(rows, TK)
            m_new = jnp.maximum(m_i, jnp.max(s, axis=-1, keepdims=True))
            p_t = jnp.exp2(s - m_new).astype(BF16)
            acc = jnp.exp2(m_i - m_new) * acc + _dot(p_t, tile(v_ref, i))
            return (m_new, acc), (side_fn(pos, side_i) if side_fn is not None else side_i)

        def group(width, first, masked, prefetch_last=True):
            def body(j, carry):
                for i in range(width):
                    carry = step(first + width * j + i, i % 2, carry, prefetch_last or i + 1 < width, masked, i)
                return carry
            return body

        s_buf[0] = _dot_nt(q_op, tile(k_ref, 0))
        carry = ((jnp.full((rows, 1), NEG, F32), jnp.zeros((rows, 2 * HEAD_DIM), F32)), side)
        if isinstance(n_plain, int):
            assert n_plain == 0 and n_masked == 2
            carry = lax.fori_loop(0, one, group(2, 0, True, prefetch_last=False), carry)
        else:
            assert n_masked == 1
            n4 = n_plain // 4
            carry = lax.fori_loop(0, n4, group(4, 0, False), carry)
            carry = lax.fori_loop(0, (n_plain % 4) // 2, group(2, 4 * n4, False), carry)
            carry = lax.fori_loop(0, n_plain % 2, group(1, n_plain - 1, False), carry)
            carry = lax.fori_loop(0, one, lambda _, c: step(n_plain, n_plain % 2, c, False, True), carry)
        acc = carry[0][1]
        return acc[:, :HEAD_DIM] * (1.0 / acc[:, HEAD_DIM:HEAD_DIM + 1]), carry[1]

    assert tq == TK == WINDOW
    older = jnp.where(qi > 0, bias_ref[1], NEG)

    assert n_sel % 2 == 0
    work, o_c = compressed_branch()
    o_w, work = run_tiles(qs, kw_ref, vw_ref, qi, -1, 0, 2, lambda i: bias_ref[0] if i == 0 else older,
                          side_fn=lambda pos, w: knock_out(w, n_sel // 2), side=work)
    sel_t = jnp.where(work == -jnp.inf, 0.0, -1.0)
    sel_rows = jnp.concatenate([sel_t.T.astype(BF16)] * NSA_GROUP, axis=0)
    q_aug = jnp.concatenate([sel_rows, qs], axis=1)

    o_s, _ = run_tiles(q_aug, ksa_ref, vs_ref, 0, 1, qi, 1, lambda i: bias_ref[0])

    gates = _sigmoid(gl_ref[0])
    heads = []
    for h in range(NSA_GROUP):
        y_h = o_c[h] * gates[:, 3 * h:3 * h + 1]
        y_h = y_h + o_s[h * tq:(h + 1) * tq] * gates[:, 3 * h + 1:3 * h + 2]
        y_h = y_h + o_w[h * tq:(h + 1) * tq] * gates[:, 3 * h + 2:3 * h + 3]
        heads.append(y_h)
    y = jnp.concatenate(heads, axis=1)
    y_ref[0] = (y * _silu(zc_ref[0])).astype(BF16)


def _nsa(q, zc, gl, kcv, ksa, vs, kw, vw, overlap, tile_bias):
    bsz, seq, _ = q.shape
    n_cmp = kcv.shape[3]
    gw = NSA_GROUP * HEAD_DIM
    n_sel = min(N_SELECT, seq // SLC_BLOCK)
    per_bg = lambda b, g, i: (b, g, 0, 0)
    return pl.pallas_call(
        functools.partial(_nsa_kernel, n_sel=n_sel),
        grid=(bsz, NSA_KV, seq // TQ),
        in_specs=[
            pl.BlockSpec((1, TQ, gw), lambda b, g, i: (b, i, g)),
            pl.BlockSpec((1, TQ, gw), lambda b, g, i: (b, i, g)),
            pl.BlockSpec((1, TQ, LANES), lambda b, g, i: (b, i, g)),
            pl.BlockSpec((1, 2, 1, n_cmp, HEAD_DIM), lambda b, g, i: (b, 0, g, 0, 0)),
            pl.BlockSpec((1, 1, seq, SEL_LANES + HEAD_DIM), per_bg, pipeline_mode=pl.Buffered(1)),
            pl.BlockSpec((1, 1, seq, 2 * HEAD_DIM), per_bg, pipeline_mode=pl.Buffered(1)),
            pl.BlockSpec((1, 1, seq, HEAD_DIM), per_bg, pipeline_mode=pl.Buffered(1)),
            pl.BlockSpec((1, 1, seq, 2 * HEAD_DIM), per_bg, pipeline_mode=pl.Buffered(1)),
            pl.BlockSpec(overlap.shape, lambda b, g, i: (0, 0), pipeline_mode=pl.Buffered(1)),
            pl.BlockSpec(tile_bias.shape, lambda b, g, i: (0, 0, 0), pipeline_mode=pl.Buffered(1)),
        ],
        out_specs=pl.BlockSpec((1, TQ, gw), lambda b, g, i: (b, i, g)),
        out_shape=jax.ShapeDtypeStruct((bsz, seq, NSA_WIDTH), BF16),
        scratch_shapes=[pltpu.VMEM((2, NSA_GROUP * TQ, TK), F32)],
        compiler_params=pltpu.CompilerParams(
            dimension_semantics=("arbitrary", "arbitrary", "arbitrary"), vmem_limit_bytes=VMEM_LIMIT),
        name="nsa",
    )(q, zc, gl, kcv, ksa, vs, kw, vw, overlap, tile_bias)


def _final_kernel(x_ref, part_ref, yc_ref, wout_ref, g_ref, b_ref, o_ref, *, alpha):
    o_ref[0] = _layer_output(x_ref[0], part_ref[0], yc_ref[0], wout_ref[...], g_ref[...], b_ref[...], alpha)


def _final(x, part, yc, wout_c, ln_g, ln_b, alpha):
    bsz, seq, dm = x.shape
    ts = TS_PROJ
    tok = lambda b, s: (b, s, 0)
    return pl.pallas_call(
        functools.partial(_final_kernel, alpha=alpha),
        grid=(bsz, seq // ts),
        in_specs=[
            pl.BlockSpec((1, ts, dm), tok),
            pl.BlockSpec((1, ts, dm), tok),
            pl.BlockSpec((1, ts, NSA_WIDTH), tok),
            pl.BlockSpec(wout_c.shape, lambda b, s: (0, 0)),
            pl.BlockSpec(ln_g.shape, lambda b, s: (0, 0)),
            pl.BlockSpec(ln_b.shape, lambda b, s: (0, 0)),
        ],
        out_specs=pl.BlockSpec((1, ts, dm), tok),
        out_shape=jax.ShapeDtypeStruct((bsz, seq, dm), F32),
        compiler_params=pltpu.CompilerParams(
            dimension_semantics=("arbitrary", "arbitrary"), vmem_limit_bytes=VMEM_LIMIT),
        name="final",
    )(x, part, yc, wout_c, ln_g, ln_b)


def _pack_w_in(w_in):
    depth, dm, _ = w_in.shape
    gate0 = COL_C + 2 * NSA_WIDTH + 6 * KV_WIDTH
    per_group = N_GATES // NSA_KV
    gates = w_in[:, :, gate0:gate0 + N_GATES]
    gate_blocks = []
    for g in range(NSA_KV):
        blk = gates[:, :, g * per_group:(g + 1) * per_group]
        gate_blocks.append(jnp.pad(blk, ((0, 0), (0, 0), (0, LANES - per_group))))
    d_cols = w_in[:, :, gate0 + N_GATES:]
    return jnp.concatenate([w_in[:, :, :gate0]] + gate_blocks + [d_cols], axis=-1).astype(BF16)


def _block_diag_gates(wa, wx):
    depth = wa.shape[0]
    per_half = LRU_BLOCKS // 2
    half = per_half * LRU_BLOCK
    halves = []
    for hf in range(2):
        mats = []
        for w in (wa, wx):
            m = jnp.zeros((depth, half, half), F32)
            for j in range(per_half):
                m = m.at[:, j * LRU_BLOCK:(j + 1) * LRU_BLOCK, j * LRU_BLOCK:(j + 1) * LRU_BLOCK].set(
                    w[:, hf * per_half + j])
            mats.append(m)
        halves.append(jnp.concatenate(mats, axis=-1))
    return jnp.stack(halves, axis=1).astype(BF16)


def _overlap_matrix(n_cmp_pad, n_cmp, n_slc):
    ci = np.arange(n_cmp_pad)[:, None] * CMP_STRIDE
    sj = np.arange(SEL_LANES)[None, :] * SLC_BLOCK
    ov = (ci < sj + SLC_BLOCK) & (ci + CMP_LEN > sj)
    ov &= (np.arange(n_cmp_pad)[:, None] < n_cmp) & (np.arange(SEL_LANES)[None, :] < n_slc)
    return jnp.asarray(ov.T.astype(np.float32), dtype=BF16)


def _tile_bias():
    r = np.arange(TQ)[:, None]
    k = np.arange(TK)[None, :]
    diag = np.where(k <= r, 0.0, NEG)
    older = np.where(k > r, 0.0, NEG)
    return jnp.asarray(np.stack([diag, older]), dtype=F32)


def kernel(x, mem, w_in, sgu_ln_g, sgu_ln_b, sgu_w, sgu_b, conv_w, conv_b, lru_wa, lru_ba, lru_wx, lru_bx,
           lru_lambda, cmp_pos, cmp_w1, cmp_w2, w_mem_kv, w_out, ln_g, ln_b):
    depth = w_in.shape[0]
    bsz, seq, dm = x.shape
    assert TQ == TK and WINDOW <= TK
    assert seq % TK == 0 and seq // SLC_BLOCK <= SEL_LANES
    alpha = (2 * depth) ** 0.25
    n_chunk = seq // CMP_STRIDE
    n_cmp = (seq - CMP_LEN) // CMP_STRIDE + 1

    w_in_p = _pack_w_in(w_in)
    wbd = _block_diag_gates(lru_wa, lru_wx)
    wmem = w_mem_kv.astype(BF16)
    c0 = A_WIDTH + LRU_WIDTH
    wout_abd = jnp.concatenate([w_out[:, :c0], w_out[:, c0 + NSA_WIDTH:]], axis=1).astype(BF16)
    wout_c = w_out[:, c0:c0 + NSA_WIDTH].astype(BF16)
    w1 = cmp_w1.astype(BF16)
    w2 = cmp_w2.astype(BF16)
    pos_flat = cmp_pos.reshape(depth, 2, 1, CMP_LEN * HEAD_DIM)
    overlap = _overlap_matrix(n_chunk, n_cmp, seq // SLC_BLOCK)
    tile_bias = _tile_bias()
    row = lambda a: a[:, None, :]

    prev = None
    for l in range(depth):
        outs = _proj_mix(
            x, prev, alpha, mem, w_in_p[l], row(sgu_ln_g)[l], row(sgu_ln_b)[l], sgu_w[l], sgu_b[l][:, :, None],
            conv_w[l], row(conv_b)[l], wbd[l], row(lru_ba)[l], row(lru_bx)[l], row(lru_lambda)[l],
            wmem[l], wout_abd[l])
        if prev is not None:
            x, outs = outs[0], outs[1:]
        part, q, zc, gl, kvc_raw, ksa, vs, kw, vw = outs
        fc = kvc_raw.reshape(bsz, 2, NSA_KV, n_chunk, CMP_STRIDE * HEAD_DIM)
        kcv = _compress(fc, pos_flat[l], w1[l], w2[l])
        yc = _nsa(q, zc, gl, kcv, ksa, vs, kw, vw, overlap, tile_bias)
        prev = (part, yc, wout_c[l], row(ln_g)[l], row(ln_b)[l])
    return _final(x, *prev, alpha)
```

```python
import functools

import jax
import jax.numpy as jnp
import numpy as np
from jax import lax
from jax.experimental import pallas as pl
from jax.experimental.pallas import tpu as pltpu

F32 = jnp.float32
BF16 = jnp.bfloat16

CHUNK = 128
A_GROUPS = 4
A_WIDTH = 512
LRU_WIDTH = 512
LRU_BLOCKS = 8
LRU_BLOCK = 64
CONV_W = 4
C_LRU = 8.0
HEAD_DIM = 64
NSA_KV = 2
NSA_GROUP = 4
NSA_WIDTH = 512
KV_WIDTH = 128
CMP_LEN = 32
CMP_STRIDE = 16
CMP_HIDDEN = 128
SLC_BLOCK = 64
N_SELECT = 16
WINDOW = 512
N_GATES = 24
MEM_HEADS = 4
MEM_WIDTH = 256
LN_EPS = 1e-5
NEG = -1e30
FORCE_BONUS = 1e4

LANES = 128
SEL_LANES = 128
VMEM_LIMIT = 56 * 1024 * 1024

COL_A = 0
COL_B = 1536
COL_C = 2560
COL_G = 4352
COL_D = 4608
D_IN_P = 5120

TS_PROJ = 512
TQ = 512
TK = 512

Q_SCALE = HEAD_DIM ** -0.5 * float(np.log2(np.e))


def _sigmoid(x):
    return 1.0 / (1.0 + jnp.exp(-x))


def _silu(x):
    return x * _sigmoid(x)


def _gelu(x):
    return jax.nn.gelu(x, approximate=True)


def _dot(a, b):
    return jnp.dot(a, b, preferred_element_type=F32)


def _dot_nt(a, b):
    return lax.dot_general(a, b, (((1,), (1,)), ((), ())), preferred_element_type=F32)


def _layer_output(x, part, yc, wout_c, g, b, alpha):
    y = alpha * x + (part + _dot(yc, wout_c))
    mu = jnp.mean(y, axis=-1, keepdims=True)
    var = jnp.mean(jnp.square(y - mu), axis=-1, keepdims=True)
    return ((y - mu) * lax.rsqrt(var + LN_EPS)) * g + b


def _proj_mix_kernel(*refs, alpha):
    if alpha is not None:
        x_ref, part_p_ref, yc_p_ref, woutc_p_ref, lng_p_ref, lnb_p_ref = refs[:6]
        refs = refs[6:]
    else:
        x_ref, refs = refs[0], refs[1:]
    (mem_ref, w_in_ref, sln_g_ref, sln_b_ref, sw_ref, sb_ref, cw_ref, cb_ref, wbd_ref, ba_ref, bx_ref, lam_ref,
     wmem_ref, wout_ref) = refs[:14]
    refs = refs[14:]
    if alpha is not None:
        xo_ref, refs = refs[0], refs[1:]
    (part_ref, q_ref, zc_ref, gl_ref, kvc_ref, ksa_ref, vs_ref, kw_ref, vw_ref,
     conv_buf, h_carry, memk, memv) = refs
    s = pl.program_id(1)
    ts = x_ref.shape[1]
    if alpha is not None:
        x_new = _layer_output(x_ref[0], part_p_ref[0], yc_p_ref[0], woutc_p_ref[...], lng_p_ref[...],
                              lnb_p_ref[...], alpha)
        xo_ref[0] = x_new
        xb = x_new.astype(BF16)
    else:
        xb = x_ref[0].astype(BF16)

    @pl.when(s == 0)
    def _():
        conv_buf[0:8, :] = jnp.zeros((8, LRU_WIDTH), F32)
        h_carry[...] = jnp.zeros_like(h_carry)
        kv = _dot(mem_ref[0].astype(BF16), wmem_ref[...])
        memk[...] = kv[:, :MEM_WIDTH].astype(BF16)
        memv[...] = kv[:, MEM_WIDTH:].astype(BF16)

    pa = _dot(xb, w_in_ref[:, COL_A:COL_A + 3 * A_WIDTH])
    u = _gelu(pa[:, 0:A_WIDTH])
    v = _gelu(pa[:, A_WIDTH:2 * A_WIDTH])
    za = pa[:, 2 * A_WIDTH:3 * A_WIDTH]
    row = lax.broadcasted_iota(jnp.int32, (CHUNK, CHUNK), 0)
    col = lax.broadcasted_iota(jnp.int32, (CHUNK, CHUNK), 1)
    causal = col <= row
    sv_groups = []
    for g in range(A_GROUPS):
        vg = v[:, g * CHUNK:(g + 1) * CHUNK]
        mu = jnp.mean(vg, axis=-1, keepdims=True)
        var = jnp.mean(jnp.square(vg - mu), axis=-1, keepdims=True)
        vn = (vg - mu) * lax.rsqrt(var + LN_EPS)
        vn = vn * sln_g_ref[:, g * CHUNK:(g + 1) * CHUNK] + sln_b_ref[:, g * CHUNK:(g + 1) * CHUNK]
        vnb = vn.astype(BF16)
        wg = jnp.where(causal, sw_ref[g], 0.0).astype(BF16)
        bias = sb_ref[g]
        chunks = []
        for c in range(ts // CHUNK):
            chunks.append(_dot(wg, vnb[c * CHUNK:(c + 1) * CHUNK]) + bias)
        sv_groups.append(jnp.concatenate(chunks, axis=0))
    sv = jnp.concatenate(sv_groups, axis=1)
    ya = (u * sv) * _silu(za)

    pb = _dot(xb, w_in_ref[:, COL_B:COL_B + 2 * LRU_WIDTH])
    xbv = pb[:, :LRU_WIDTH]
    zb = pb[:, LRU_WIDTH:]
    conv_buf[8:8 + ts, :] = xbv
    xc = cb_ref[...] + conv_buf[5:5 + ts, :] * cw_ref[0:1, :]
    xc = xc + conv_buf[6:6 + ts, :] * cw_ref[1:2, :]
    xc = xc + conv_buf[7:7 + ts, :] * cw_ref[2:3, :]
    xc = xc + xbv * cw_ref[3:4, :]
    conv_buf[0:8, :] = xbv[ts - 8:ts, :]
    xcb = xc.astype(BF16)
    half = LRU_WIDTH // 2
    g0 = _dot(xcb[:, :half], wbd_ref[0])
    g1 = _dot(xcb[:, half:], wbd_ref[1])
    r = _sigmoid(jnp.concatenate([g0[:, :half], g1[:, :half]], axis=1) + ba_ref[...])
    ig = _sigmoid(jnp.concatenate([g0[:, half:], g1[:, half:]], axis=1) + bx_ref[...])
    nlam = -lam_ref[...]
    softplus = jnp.maximum(nlam, 0.0) + jnp.log1p(jnp.exp(-jnp.abs(nlam)))
    log_a = (-C_LRU * r) * softplus
    a_cum = jnp.exp(log_a)
    th = jnp.tanh(log_a)
    b_cum = jnp.sqrt((-2.0 * th) / (1.0 - th)) * (ig * xc)
    d = 1
    while d < ts:
        a_sh = jnp.concatenate([jnp.ones((d, LRU_WIDTH), F32), a_cum[:ts - d]], axis=0)
        b_sh = jnp.concatenate([jnp.zeros((d, LRU_WIDTH), F32), b_cum[:ts - d]], axis=0)
        b_cum = b_cum + a_cum * b_sh
        a_cum = a_cum * a_sh
        d *= 2
    h = b_cum + a_cum * h_carry[...]
    h_carry[...] = h[ts - 1:ts, :]
    yb = h * _silu(zb)

    pd = _dot(xb, w_in_ref[:, COL_D:COL_D + 2 * MEM_WIDTH])
    qd = pd[:, :MEM_WIDTH] * (HEAD_DIM ** -0.5)
    zd = pd[:, MEM_WIDTH:]
    head_of_lane = lax.broadcasted_iota(jnp.int32, (1, MEM_WIDTH), 1) // HEAD_DIM
    od = jnp.zeros((ts, MEM_WIDTH), F32)
    for hd in range(MEM_HEADS):
        hm = head_of_lane == hd
        qh = jnp.where(hm, qd, 0.0).astype(BF16)
        sc = _dot_nt(qh, memk[...])
        m = jnp.max(sc, axis=-1, keepdims=True)
        p = jnp.exp(sc - m)
        l = jnp.sum(p, axis=-1, keepdims=True)
        pv = _dot(p.astype(BF16), memv[...])
        od = od + jnp.where(hm, pv * (1.0 / l), 0.0)
    yd = od * _silu(zd)

    y_abd = jnp.concatenate([ya, yb, yd], axis=1).astype(BF16)
    part_ref[0] = _dot(y_abd, wout_ref[...])

    pc = _dot(xb, w_in_ref[:, COL_C:COL_C + 2 * NSA_WIDTH + 6 * KV_WIDTH])
    q_ref[0] = (pc[:, :NSA_WIDTH] * Q_SCALE).astype(BF16)
    zc_ref[0] = pc[:, NSA_WIDTH:2 * NSA_WIDTH]
    gl_ref[0] = _dot(xb, w_in_ref[:, COL_G:COL_G + NSA_KV * LANES])
    base = 2 * NSA_WIDTH
    kpos = s * ts + lax.broadcasted_iota(jnp.int32, (ts, SEL_LANES), 0)
    blk = lax.broadcasted_iota(jnp.int32, (ts, SEL_LANES), 1)
    sel_rows = jnp.where((kpos // SLC_BLOCK) == blk, -NEG, 0.0).astype(BF16)
    ones_col = jnp.where(lax.broadcasted_iota(jnp.int32, (ts, HEAD_DIM), 1) == 0, 1.0, 0.0).astype(BF16)
    for g in range(NSA_KV):
        lo = g * HEAD_DIM
        kvc_ref[0, 0, g] = pc[:, base + lo:base + lo + HEAD_DIM].astype(BF16)
        kvc_ref[0, 1, g] = pc[:, base + KV_WIDTH + lo:base + KV_WIDTH + lo + HEAD_DIM].astype(BF16)
        ksa_ref[0, g, :, 0:SEL_LANES] = sel_rows
        ksa_ref[0, g, :, SEL_LANES:SEL_LANES + HEAD_DIM] = (
            pc[:, base + 2 * KV_WIDTH + lo:base + 2 * KV_WIDTH + lo + HEAD_DIM].astype(BF16))
        vs_ref[0, g, :, 0:HEAD_DIM] = (
            pc[:, base + 3 * KV_WIDTH + lo:base + 3 * KV_WIDTH + lo + HEAD_DIM].astype(BF16))
        vs_ref[0, g, :, HEAD_DIM:2 * HEAD_DIM] = ones_col
        kw_ref[0, g] = pc[:, base + 4 * KV_WIDTH + lo:base + 4 * KV_WIDTH + lo + HEAD_DIM].astype(BF16)
        vw_ref[0, g, :, 0:HEAD_DIM] = (
            pc[:, base + 5 * KV_WIDTH + lo:base + 5 * KV_WIDTH + lo + HEAD_DIM].astype(BF16))
        vw_ref[0, g, :, HEAD_DIM:2 * HEAD_DIM] = ones_col


def _proj_mix(x, prev, alpha, mem, w_in_p, sln_g, sln_b, sw, sb, cw, cb, wbd, ba, bx, lam, wmem, wout_abd):
    bsz, seq, dm = x.shape
    ts = TS_PROJ
    n_mem = mem.shape[1]
    tok = lambda b, s: (b, s, 0)

    def whole(a):
        return pl.BlockSpec(a.shape, lambda b, s: (0,) * a.ndim, pipeline_mode=pl.Buffered(1))

    consts = (w_in_p, sln_g, sln_b, sw, sb, cw, cb, wbd, ba, bx, lam, wmem, wout_abd)
    args = [x]
    in_specs = [pl.BlockSpec((1, ts, dm), tok)]
    if prev is not None:
        part_p, yc_p, wout_c_p, ln_g_p, ln_b_p = prev
        args += [part_p, yc_p, wout_c_p, ln_g_p, ln_b_p]
        in_specs += [pl.BlockSpec((1, ts, dm), tok), pl.BlockSpec((1, ts, NSA_WIDTH), tok),
                     whole(wout_c_p), whole(ln_g_p), whole(ln_b_p)]
    args += [mem, *consts]
    in_specs += [pl.BlockSpec((1, n_mem, dm), lambda b, s: (b, 0, 0), pipeline_mode=pl.Buffered(1))]
    in_specs += [whole(a) for a in consts]
    kv_spec = pl.BlockSpec((1, NSA_KV, ts, HEAD_DIM), lambda b, s: (b, 0, s, 0))
    va_spec = pl.BlockSpec((1, NSA_KV, ts, 2 * HEAD_DIM), lambda b, s: (b, 0, s, 0))
    out_specs = [
        pl.BlockSpec((1, ts, dm), lambda b, s: (b, s, 0)),
        pl.BlockSpec((1, ts, NSA_WIDTH), lambda b, s: (b, s, 0)),
        pl.BlockSpec((1, ts, NSA_WIDTH), lambda b, s: (b, s, 0)),
        pl.BlockSpec((1, ts, NSA_KV * LANES), lambda b, s: (b, s, 0)),
        pl.BlockSpec((1, 2, NSA_KV, ts, HEAD_DIM), lambda b, s: (b, 0, 0, s, 0)),
        pl.BlockSpec((1, NSA_KV, ts, SEL_LANES + HEAD_DIM), lambda b, s: (b, 0, s, 0)),
        va_spec, kv_spec, va_spec,
    ]
    out_shape = [
        jax.ShapeDtypeStruct((bsz, seq, dm), F32),
        jax.ShapeDtypeStruct((bsz, seq, NSA_WIDTH), BF16),
        jax.ShapeDtypeStruct((bsz, seq, NSA_WIDTH), F32),
        jax.ShapeDtypeStruct((bsz, seq, NSA_KV * LANES), F32),
        jax.ShapeDtypeStruct((bsz, 2, NSA_KV, seq, HEAD_DIM), BF16),
        jax.ShapeDtypeStruct((bsz, NSA_KV, seq, SEL_LANES + HEAD_DIM), BF16),
        jax.ShapeDtypeStruct((bsz, NSA_KV, seq, 2 * HEAD_DIM), BF16),
        jax.ShapeDtypeStruct((bsz, NSA_KV, seq, HEAD_DIM), BF16),
        jax.ShapeDtypeStruct((bsz, NSA_KV, seq, 2 * HEAD_DIM), BF16),
    ]
    if prev is not None:
        out_specs = [pl.BlockSpec((1, ts, dm), tok)] + out_specs
        out_shape = [jax.ShapeDtypeStruct((bsz, seq, dm), F32)] + out_shape
    return pl.pallas_call(
        functools.partial(_proj_mix_kernel, alpha=alpha if prev is not None else None),
        grid=(bsz, seq // ts),
        in_specs=in_specs,
        out_specs=out_specs,
        out_shape=out_shape,
        scratch_shapes=[
            pltpu.VMEM((ts + 8, LRU_WIDTH), F32),
            pltpu.VMEM((1, LRU_WIDTH), F32),
            pltpu.VMEM((n_mem, MEM_WIDTH), BF16),
            pltpu.VMEM((n_mem, MEM_WIDTH), BF16),
        ],
        compiler_params=pltpu.CompilerParams(
            dimension_semantics=("arbitrary", "arbitrary"), vmem_limit_bytes=VMEM_LIMIT),
        name="proj_mix",
    )(*args)


def _compress_kernel(fc_ref, pos_ref, w1_ref, w2_ref, out_ref):
    fc = fc_ref[0, 0, 0].astype(F32)
    half = (CMP_LEN // 2) * HEAD_DIM
    pos = pos_ref[0]
    fa = (fc + pos[:, :half]).astype(BF16)
    fb = (fc + pos[:, half:]).astype(BF16)
    ha = _dot(fa, w1_ref[0, :half, :])
    hb = _dot(fb, w1_ref[0, half:, :])
    hb_next = jnp.concatenate([hb[1:], hb[:1]], axis=0)
    hid = _gelu(ha + hb_next)
    out_ref[0, 0, 0] = _dot(hid.astype(BF16), w2_ref[0]).astype(BF16)


def _compress(fc, pos_flat, w1, w2):
    bsz, _, _, n_chunk, width = fc.shape
    return pl.pallas_call(
        _compress_kernel,
        grid=(bsz, 2, NSA_KV),
        in_specs=[
            pl.BlockSpec((1, 1, 1, n_chunk, width), lambda b, k, g: (b, k, g, 0, 0)),
            pl.BlockSpec((1, 1, CMP_LEN * HEAD_DIM), lambda b, k, g: (k, 0, 0)),
            pl.BlockSpec((1, CMP_LEN * HEAD_DIM, CMP_HIDDEN), lambda b, k, g: (k, 0, 0)),
            pl.BlockSpec((1, CMP_HIDDEN, HEAD_DIM), lambda b, k, g: (k, 0, 0)),
        ],
        out_specs=pl.BlockSpec((1, 1, 1, n_chunk, HEAD_DIM), lambda b, k, g: (b, k, g, 0, 0)),
        out_shape=jax.ShapeDtypeStruct((bsz, 2, NSA_KV, n_chunk, HEAD_DIM), BF16),
        compiler_params=pltpu.CompilerParams(
            dimension_semantics=("arbitrary", "arbitrary", "arbitrary"), vmem_limit_bytes=VMEM_LIMIT),
        name="compress",
    )(fc, pos_flat, w1, w2)


def _nsa_kernel(q_ref, zc_ref, gl_ref, kcv_ref, ksa_ref, vs_ref, kw_ref, vw_ref, ovt_ref, bias_ref, y_ref, s_buf,
                m_ref, acc_ref, *, n_sel):
    qi = pl.program_id(2)
    tq = q_ref.shape[1]
    rows = NSA_GROUP * tq
    s0 = qi * tq
    q = q_ref[0]
    q_heads = [q[:, h * HEAD_DIM:(h + 1) * HEAD_DIM] for h in range(NSA_GROUP)]
    qs = jnp.concatenate(q_heads, axis=0)
    t_q = s0 + lax.broadcasted_iota(jnp.int32, (tq, 1), 0)

    blk = lax.broadcasted_iota(jnp.int32, (SEL_LANES, 1), 0)
    blk_f = blk.astype(F32)

    def compressed_branch():
        kc = kcv_ref[0, 0, 0]
        vc = kcv_ref[0, 1, 0]
        n_cmp = kc.shape[0]
        cmp_end = lax.broadcasted_iota(jnp.int32, (1, n_cmp), 1) * CMP_STRIDE + (CMP_LEN - 1)
        valid_c = cmp_end <= t_q
        any_valid = t_q >= CMP_LEN - 1
        o_c = []
        ps = None
        for h in range(NSA_GROUP):
            sc = jnp.where(valid_c, _dot_nt(q_heads[h], kc), NEG)
            m = jnp.max(sc, axis=-1, keepdims=True)
            p = jnp.exp2(sc - m)
            l = jnp.sum(p, axis=-1, keepdims=True)
            pn = p * jnp.where(any_valid, 1.0 / l, 0.0)
            o_c.append(_dot(pn.astype(BF16), vc))
            ps = pn if ps is None else ps + pn
        ps_hi = ps.astype(BF16)
        ps_lo = (ps - ps_hi.astype(F32)).astype(BF16)
        imp = _dot_nt(ovt_ref[...], ps_hi) + _dot_nt(ovt_ref[...], ps_lo)
        cur = (s0 + lax.broadcasted_iota(jnp.int32, (1, tq), 1)) // SLC_BLOCK
        future = blk > cur
        forced = (blk == 0) | (blk == cur) | (blk == cur - 1)
        return jnp.where(future, -1.0, imp + jnp.where(forced, FORCE_BONUS, 0.0)), tuple(o_c)

    def knock_out(work_i, rounds):
        for _ in range(rounds):
            mx = jnp.max(work_i, axis=0, keepdims=True)
            idx = jnp.min(jnp.where(work_i == mx, blk_f, float(SEL_LANES)), axis=0, keepdims=True)
            work_i = jnp.where(blk_f == idx, -jnp.inf, work_i)
        return work_i

    n_tiles = ksa_ref.shape[2] // TK

    def tile_start(kt):
        return pl.multiple_of(jnp.clip(kt, 0, n_tiles - 1) * TK, TK)

    one = 1 + jnp.minimum(qi, 0)

    def run_tiles(q_op, k_ref, v_ref, kt0, stride, n_plain, n_masked, bias_fn, side_fn=None, side=None):
        def tile(ref, i):
            return ref[0, 0, pl.ds(tile_start(kt0 + stride * i), TK), :]

        def step(i, slot, state, prefetch, masked, pos=0):
            m_i, acc, side_i = state
            s = s_buf[slot]
            if prefetch:
                s_buf[1 - slot] = _dot_nt(q_op, tile(k_ref, i + 1))
            if masked:
                s = (s.reshape(NSA_GROUP, tq, TK) + bias_fn(pos)[None]).reshape(rows, TK)
            m_new = jnp.maximum(m_i, jnp.broadcast_to(jnp.max(s, axis=-1, keepdims=True), m_i.shape))
            p_t = jnp.exp2(s - jnp.concatenate([m_new] * (TK // LANES), axis=1)).astype(BF16)
            acc = jnp.exp2(m_i - m_new) * acc + _dot(p_t, tile(v_ref, i))
            return m_new, acc, (side_fn(pos, side_i) if side_fn is not None else side_i)

        def group(width, first, masked, prefetch_last=True, slot0=0):
            def body(j, side_i):
                state = (m_ref[...], acc_ref[...], side_i)
                for i in range(width):
                    slot = slot0 if width == 1 else i % 2
                    state = step(first + width * j + i, slot, state, prefetch_last or i + 1 < width, masked, i)
                m_ref[...] = state[0]
                acc_ref[...] = state[1]
                return state[2]
            return body

        s_buf[0] = _dot_nt(q_op, tile(k_ref, 0))
        m_ref[...] = jnp.full(m_ref.shape, NEG, F32)
        acc_ref[...] = jnp.zeros(acc_ref.shape, F32)
        if isinstance(n_plain, int):
            assert n_plain == 0 and n_masked == 2
            side = lax.fori_loop(0, one, group(2, 0, True, prefetch_last=False), side)
        else:
            assert n_masked == 1
            n4 = n_plain // 4
            side = lax.fori_loop(0, n4, group(4, 0, False), side)
            side = lax.fori_loop(0, (n_plain % 4) // 2, group(2, 4 * n4, False), side)
            side = lax.fori_loop(0, n_plain % 2, group(1, n_plain - 1, False), side)
            side = lax.fori_loop(0, one, group(1, n_plain, True, prefetch_last=False, slot0=n_plain % 2), side)
        acc = acc_ref[...]
        return acc[:, :HEAD_DIM] * (1.0 / acc[:, HEAD_DIM:HEAD_DIM + 1]), side

    assert tq == TK == WINDOW
    older = jnp.where(qi > 0, bias_ref[1], NEG)

    assert n_sel % 2 == 0
    work, o_c = compressed_branch()
    o_w, work = run_tiles(qs, kw_ref, vw_ref, qi, -1, 0, 2, lambda i: bias_ref[0] if i == 0 else older,
                          side_fn=lambda pos, w: knock_out(w, n_sel // 2), side=work)
    sel_t = jnp.where(work == -jnp.inf, 0.0, -1.0)
    sel_rows = jnp.concatenate([sel_t.T.astype(BF16)] * NSA_GROUP, axis=0)
    q_aug = jnp.concatenate([sel_rows, qs], axis=1)

    o_s, _ = run_tiles(q_aug, ksa_ref, vs_ref, 0, 1, qi, 1, lambda i: bias_ref[0])

    gates = _sigmoid(gl_ref[0])
    heads = []
    for h in range(NSA_GROUP):
        y_h = o_c[h] * gates[:, 3 * h:3 * h + 1]
        y_h = y_h + o_s[h * tq:(h + 1) * tq] * gates[:, 3 * h + 1:3 * h + 2]
        y_h = y_h + o_w[h * tq:(h + 1) * tq] * gates[:, 3 * h + 2:3 * h + 3]
        heads.append(y_h)
    y = jnp.concatenate(heads, axis=1)
    y_ref[0] = (y * _silu(zc_ref[0])).astype(BF16)


def _nsa(q, zc, gl, kcv, ksa, vs, kw, vw, overlap, tile_bias):
    bsz, seq, _ = q.shape
    n_cmp = kcv.shape[3]
    gw = NSA_GROUP * HEAD_DIM
    n_sel = min(N_SELECT, seq // SLC_BLOCK)
    per_bg = lambda b, g, i: (b, g, 0, 0)
    return pl.pallas_call(
        functools.partial(_nsa_kernel, n_sel=n_sel),
        grid=(bsz, NSA_KV, seq // TQ),
        in_specs=[
            pl.BlockSpec((1, TQ, gw), lambda b, g, i: (b, i, g)),
            pl.BlockSpec((1, TQ, gw), lambda b, g, i: (b, i, g)),
            pl.BlockSpec((1, TQ, LANES), lambda b, g, i: (b, i, g)),
            pl.BlockSpec((1, 2, 1, n_cmp, HEAD_DIM), lambda b, g, i: (b, 0, g, 0, 0)),
            pl.BlockSpec((1, 1, seq, SEL_LANES + HEAD_DIM), per_bg, pipeline_mode=pl.Buffered(1)),
            pl.BlockSpec((1, 1, seq, 2 * HEAD_DIM), per_bg, pipeline_mode=pl.Buffered(1)),
            pl.BlockSpec((1, 1, seq, HEAD_DIM), per_bg, pipeline_mode=pl.Buffered(1)),
            pl.BlockSpec((1, 1, seq, 2 * HEAD_DIM), per_bg, pipeline_mode=pl.Buffered(1)),
            pl.BlockSpec(overlap.shape, lambda b, g, i: (0, 0), pipeline_mode=pl.Buffered(1)),
            pl.BlockSpec(tile_bias.shape, lambda b, g, i: (0, 0, 0), pipeline_mode=pl.Buffered(1)),
        ],
        out_specs=pl.BlockSpec((1, TQ, gw), lambda b, g, i: (b, i, g)),
        out_shape=jax.ShapeDtypeStruct((bsz, seq, NSA_WIDTH), BF16),
        scratch_shapes=[pltpu.VMEM((2, NSA_GROUP * TQ, TK), F32),
                        pltpu.VMEM((NSA_GROUP * TQ, LANES), F32),
                        pltpu.VMEM((NSA_GROUP * TQ, 2 * HEAD_DIM), F32)],
        compiler_params=pltpu.CompilerParams(
            dimension_semantics=("arbitrary", "arbitrary", "arbitrary"), vmem_limit_bytes=VMEM_LIMIT),
        name="nsa",
    )(q, zc, gl, kcv, ksa, vs, kw, vw, overlap, tile_bias)


def _final_kernel(x_ref, part_ref, yc_ref, wout_ref, g_ref, b_ref, o_ref, *, alpha):
    o_ref[0] = _layer_output(x_ref[0], part_ref[0], yc_ref[0], wout_ref[...], g_ref[...], b_ref[...], alpha)


def _final(x, part, yc, wout_c, ln_g, ln_b, alpha):
    bsz, seq, dm = x.shape
    ts = TS_PROJ
    tok = lambda b, s: (b, s, 0)
    return pl.pallas_call(
        functools.partial(_final_kernel, alpha=alpha),
        grid=(bsz, seq // ts),
        in_specs=[
            pl.BlockSpec((1, ts, dm), tok),
            pl.BlockSpec((1, ts, dm), tok),
            pl.BlockSpec((1, ts, NSA_WIDTH), tok),
            pl.BlockSpec(wout_c.shape, lambda b, s: (0, 0)),
            pl.BlockSpec(ln_g.shape, lambda b, s: (0, 0)),
            pl.BlockSpec(ln_b.shape, lambda b, s: (0, 0)),
        ],
        out_specs=pl.BlockSpec((1, ts, dm), tok),
        out_shape=jax.ShapeDtypeStruct((bsz, seq, dm), F32),
        compiler_params=pltpu.CompilerParams(
            dimension_semantics=("arbitrary", "arbitrary"), vmem_limit_bytes=VMEM_LIMIT),
        name="final",
    )(x, part, yc, wout_c, ln_g, ln_b)


def _pack_w_in(w_in):
    depth, dm, _ = w_in.shape
    gate0 = COL_C + 2 * NSA_WIDTH + 6 * KV_WIDTH
    per_group = N_GATES // NSA_KV
    gates = w_in[:, :, gate0:gate0 + N_GATES]
    gate_blocks = []
    for g in range(NSA_KV):
        blk = gates[:, :, g * per_group:(g + 1) * per_group]
        gate_blocks.append(jnp.pad(blk, ((0, 0), (0, 0), (0, LANES - per_group))))
    d_cols = w_in[:, :, gate0 + N_GATES:]
    return jnp.concatenate([w_in[:, :, :gate0]] + gate_blocks + [d_cols], axis=-1).astype(BF16)


def _block_diag_gates(wa, wx):
    depth = wa.shape[0]
    per_half = LRU_BLOCKS // 2
    half = per_half * LRU_BLOCK
    halves = []
    for hf in range(2):
        mats = []
        for w in (wa, wx):
            m = jnp.zeros((depth, half, half), F32)
            for j in range(per_half):
                m = m.at[:, j * LRU_BLOCK:(j + 1) * LRU_BLOCK, j * LRU_BLOCK:(j + 1) * LRU_BLOCK].set(
                    w[:, hf * per_half + j])
            mats.append(m)
        halves.append(jnp.concatenate(mats, axis=-1))
    return jnp.stack(halves, axis=1).astype(BF16)


def _overlap_matrix(n_cmp_pad, n_cmp, n_slc):
    ci = np.arange(n_cmp_pad)[:, None] * CMP_STRIDE
    sj = np.arange(SEL_LANES)[None, :] * SLC_BLOCK
    ov = (ci < sj + SLC_BLOCK) & (ci + CMP_LEN > sj)
    ov &= (np.arange(n_cmp_pad)[:, None] < n_cmp) & (np.arange(SEL_LANES)[None, :] < n_slc)
    return jnp.asarray(ov.T.astype(np.float32), dtype=BF16)


def _tile_bias():
    r = np.arange(TQ)[:, None]
    k = np.arange(TK)[None, :]
    diag = np.where(k <= r, 0.0, NEG)
    older = np.where(k > r, 0.0, NEG)
    return jnp.asarray(np.stack([diag, older]), dtype=F32)


def kernel(x, mem, w_in, sgu_ln_g, sgu_ln_b, sgu_w, sgu_b, conv_w, conv_b, lru_wa, lru_ba, lru_wx, lru_bx,
           lru_lambda, cmp_pos, cmp_w1, cmp_w2, w_mem_kv, w_out, ln_g, ln_b):
    depth = w_in.shape[0]
    bsz, seq, dm = x.shape
    assert TQ == TK and WINDOW <= TK
    assert seq % TK == 0 and seq // SLC_BLOCK <= SEL_LANES
    alpha = (2 * depth) ** 0.25
    n_chunk = seq // CMP_STRIDE
    n_cmp = (seq - CMP_LEN) // CMP_STRIDE + 1

    w_in_p = _pack_w_in(w_in)
    wbd = _block_diag_gates(lru_wa, lru_wx)
    wmem = w_mem_kv.astype(BF16)
    c0 = A_WIDTH + LRU_WIDTH
    wout_abd = jnp.concatenate([w_out[:, :c0], w_out[:, c0 + NSA_WIDTH:]], axis=1).astype(BF16)
    wout_c = w_out[:, c0:c0 + NSA_WIDTH].astype(BF16)
    w1 = cmp_w1.astype(BF16)
    w2 = cmp_w2.astype(BF16)
    pos_flat = cmp_pos.reshape(depth, 2, 1, CMP_LEN * HEAD_DIM)
    overlap = _overlap_matrix(n_chunk, n_cmp, seq // SLC_BLOCK)
    tile_bias = _tile_bias()
    row = lambda a: a[:, None, :]

    prev = None
    for l in range(depth):
        outs = _proj_mix(
            x, prev, alpha, mem, w_in_p[l], row(sgu_ln_g)[l], row(sgu_ln_b)[l], sgu_w[l], sgu_b[l][:, :, None],
            conv_w[l], row(conv_b)[l], wbd[l], row(lru_ba)[l], row(lru_bx)[l], row(lru_lambda)[l],
            wmem[l], wout_abd[l])
        if prev is not None:
            x, outs = outs[0], outs[1:]
        part, q, zc, gl, kvc_raw, ksa, vs, kw, vw = outs
        fc = kvc_raw.reshape(bsz, 2, NSA_KV, n_chunk, CMP_STRIDE * HEAD_DIM)
        kcv = _compress(fc, pos_flat[l], w1[l], w2[l])
        yc = _nsa(q, zc, gl, kcv, ksa, vs, kw, vw, overlap, tile_bias)
        prev = (part, yc, wout_c[l], row(ln_g)[l], row(ln_b)[l])
    return _final(x, *prev, alpha)
```

```python
import functools

import jax
import jax.numpy as jnp
import numpy as np
from jax import lax
from jax.experimental import pallas as pl
from jax.experimental.pallas import tpu as pltpu

F32 = jnp.float32
BF16 = jnp.bfloat16

CHUNK = 128
A_GROUPS = 4
A_WIDTH = 512
LRU_WIDTH = 512
LRU_BLOCKS = 8
LRU_BLOCK = 64
CONV_W = 4
C_LRU = 8.0
HEAD_DIM = 64
NSA_KV = 2
NSA_GROUP = 4
NSA_WIDTH = 512
KV_WIDTH = 128
CMP_LEN = 32
CMP_STRIDE = 16
CMP_HIDDEN = 128
SLC_BLOCK = 64
N_SELECT = 16
WINDOW = 512
N_GATES = 24
MEM_HEADS = 4
MEM_WIDTH = 256
LN_EPS = 1e-5
NEG = -1e30
FORCE_BONUS = 1e4

LANES = 128
SUBLANES = 8
SEL_LANES = 128
VMEM_LIMIT = 56 * 1024 * 1024

COL_A = 0
COL_B = 1536
COL_C = 2560
COL_G = 4352
COL_D = 4608
D_IN_P = 5120

TS_PROJ = 512
TQ = 512
TK = 512

Q_SCALE = HEAD_DIM ** -0.5 * float(np.log2(np.e))


def _sigmoid(x):
    return 1.0 / (1.0 + jnp.exp(-x))


def _silu(x):
    return x * _sigmoid(x)


def _gelu(x):
    return jax.nn.gelu(x, approximate=True)


def _dot(a, b):
    return jnp.dot(a, b, preferred_element_type=F32)


def _dot_nt(a, b):
    return lax.dot_general(a, b, (((1,), (1,)), ((), ())), preferred_element_type=F32)


def _layer_output(x, part, yc, wout_c, g, b, alpha):
    y = alpha * x + (part + _dot(yc, wout_c))
    mu = jnp.mean(y, axis=-1, keepdims=True)
    var = jnp.mean(jnp.square(y - mu), axis=-1, keepdims=True)
    return ((y - mu) * lax.rsqrt(var + LN_EPS)) * g + b


def _proj_mix_kernel(*refs, alpha):
    if alpha is not None:
        x_ref, part_p_ref, yc_p_ref, woutc_p_ref, lng_p_ref, lnb_p_ref = refs[:6]
        refs = refs[6:]
    else:
        x_ref, refs = refs[0], refs[1:]
    (mem_ref, w_in_ref, sln_g_ref, sln_b_ref, sw_ref, sb_ref, cw_ref, cb_ref, wbd_ref, ba_ref, bx_ref, lam_ref,
     wmem_ref, wout_ref) = refs[:14]
    refs = refs[14:]
    if alpha is not None:
        xo_ref, refs = refs[0], refs[1:]
    (part_ref, q_ref, zc_ref, gl_ref, kvc_ref, ksa_ref, vs_ref, kw_ref, vw_ref,
     conv_buf, h_carry, memk, memv) = refs
    s = pl.program_id(1)
    ts = x_ref.shape[1]
    if alpha is not None:
        x_new = _layer_output(x_ref[0], part_p_ref[0], yc_p_ref[0], woutc_p_ref[...], lng_p_ref[...],
                              lnb_p_ref[...], alpha)
        xo_ref[0] = x_new
        xb = x_new.astype(BF16)
    else:
        xb = x_ref[0].astype(BF16)

    @pl.when(s == 0)
    def _():
        conv_buf[0:8, :] = jnp.zeros((8, LRU_WIDTH), F32)
        h_carry[...] = jnp.zeros_like(h_carry)
        kv = _dot(mem_ref[0].astype(BF16), wmem_ref[...])
        memk[...] = kv[:, :MEM_WIDTH].astype(BF16)
        memv[...] = kv[:, MEM_WIDTH:].astype(BF16)

    pa = _dot(xb, w_in_ref[:, COL_A:COL_A + 3 * A_WIDTH])
    u = _gelu(pa[:, 0:A_WIDTH])
    v = _gelu(pa[:, A_WIDTH:2 * A_WIDTH])
    za = pa[:, 2 * A_WIDTH:3 * A_WIDTH]
    row = lax.broadcasted_iota(jnp.int32, (CHUNK, CHUNK), 0)
    col = lax.broadcasted_iota(jnp.int32, (CHUNK, CHUNK), 1)
    causal = col <= row
    sv_groups = []
    for g in range(A_GROUPS):
        vg = v[:, g * CHUNK:(g + 1) * CHUNK]
        mu = jnp.mean(vg, axis=-1, keepdims=True)
        var = jnp.mean(jnp.square(vg - mu), axis=-1, keepdims=True)
        vn = (vg - mu) * lax.rsqrt(var + LN_EPS)
        vn = vn * sln_g_ref[:, g * CHUNK:(g + 1) * CHUNK] + sln_b_ref[:, g * CHUNK:(g + 1) * CHUNK]
        vnb = vn.astype(BF16)
        wg = jnp.where(causal, sw_ref[g], 0.0).astype(BF16)
        bias = sb_ref[g]
        chunks = []
        for c in range(ts // CHUNK):
            chunks.append(_dot(wg, vnb[c * CHUNK:(c + 1) * CHUNK]) + bias)
        sv_groups.append(jnp.concatenate(chunks, axis=0))
    sv = jnp.concatenate(sv_groups, axis=1)
    ya = (u * sv) * _silu(za)

    pb = _dot(xb, w_in_ref[:, COL_B:COL_B + 2 * LRU_WIDTH])
    xbv = pb[:, :LRU_WIDTH]
    zb = pb[:, LRU_WIDTH:]
    conv_buf[8:8 + ts, :] = xbv
    xc = cb_ref[...] + conv_buf[5:5 + ts, :] * cw_ref[0:1, :]
    xc = xc + conv_buf[6:6 + ts, :] * cw_ref[1:2, :]
    xc = xc + conv_buf[7:7 + ts, :] * cw_ref[2:3, :]
    xc = xc + xbv * cw_ref[3:4, :]
    conv_buf[0:8, :] = xbv[ts - 8:ts, :]
    xcb = xc.astype(BF16)
    half = LRU_WIDTH // 2
    g0 = _dot(xcb[:, :half], wbd_ref[0])
    g1 = _dot(xcb[:, half:], wbd_ref[1])
    r = _sigmoid(jnp.concatenate([g0[:, :half], g1[:, :half]], axis=1) + ba_ref[...])
    ig = _sigmoid(jnp.concatenate([g0[:, half:], g1[:, half:]], axis=1) + bx_ref[...])
    nlam = -lam_ref[...]
    softplus = jnp.maximum(nlam, 0.0) + jnp.log1p(jnp.exp(-jnp.abs(nlam)))
    log_a = (-C_LRU * r) * softplus
    a_cum = jnp.exp(log_a)
    th = jnp.tanh(log_a)
    b_cum = jnp.sqrt((-2.0 * th) / (1.0 - th)) * (ig * xc)
    sub = SUBLANES
    a_cum = a_cum.reshape(ts // sub, sub, LRU_WIDTH)
    b_cum = b_cum.reshape(ts // sub, sub, LRU_WIDTH)
    row_in_block = lax.broadcasted_iota(jnp.int32, (1, sub, 1), 1)
    d = 1
    while d < sub:
        keep = row_in_block >= d
        a_sh = jnp.where(keep, pltpu.roll(a_cum, d, axis=1), 1.0)
        b_sh = jnp.where(keep, pltpu.roll(b_cum, d, axis=1), 0.0)
        b_cum = b_cum + a_cum * b_sh
        a_cum = a_cum * a_sh
        d *= 2
    carry = h_carry[...]
    blocks = []
    for blk in range(ts // sub):
        hb = b_cum[blk] + a_cum[blk] * carry
        carry = hb[sub - 1:sub, :]
        blocks.append(hb)
    h = jnp.concatenate(blocks, axis=0)
    h_carry[...] = carry
    yb = h * _silu(zb)

    pd = _dot(xb, w_in_ref[:, COL_D:COL_D + 2 * MEM_WIDTH])
    qd = pd[:, :MEM_WIDTH] * (HEAD_DIM ** -0.5)
    zd = pd[:, MEM_WIDTH:]
    head_of_lane = lax.broadcasted_iota(jnp.int32, (1, MEM_WIDTH), 1) // HEAD_DIM
    od = jnp.zeros((ts, MEM_WIDTH), F32)
    for hd in range(MEM_HEADS):
        hm = head_of_lane == hd
        qh = jnp.where(hm, qd, 0.0).astype(BF16)
        sc = _dot_nt(qh, memk[...])
        m = jnp.max(sc, axis=-1, keepdims=True)
        p = jnp.exp(sc - m)
        l = jnp.sum(p, axis=-1, keepdims=True)
        pv = _dot(p.astype(BF16), memv[...])
        od = od + jnp.where(hm, pv * (1.0 / l), 0.0)
    yd = od * _silu(zd)

    y_abd = jnp.concatenate([ya, yb, yd], axis=1).astype(BF16)
    part_ref[0] = _dot(y_abd, wout_ref[...])

    pc = _dot(xb, w_in_ref[:, COL_C:COL_C + 2 * NSA_WIDTH + 6 * KV_WIDTH])
    q_ref[0] = (pc[:, :NSA_WIDTH] * Q_SCALE).astype(BF16)
    zc_ref[0] = pc[:, NSA_WIDTH:2 * NSA_WIDTH]
    gl_ref[0] = _dot(xb, w_in_ref[:, COL_G:COL_G + NSA_KV * LANES])
    base = 2 * NSA_WIDTH
    kpos = s * ts + lax.broadcasted_iota(jnp.int32, (ts, SEL_LANES), 0)
    blk = lax.broadcasted_iota(jnp.int32, (ts, SEL_LANES), 1)
    sel_rows = jnp.where((kpos // SLC_BLOCK) == blk, -NEG, 0.0).astype(BF16)
    ones_col = jnp.where(lax.broadcasted_iota(jnp.int32, (ts, HEAD_DIM), 1) == 0, 1.0, 0.0).astype(BF16)
    for g in range(NSA_KV):
        lo = g * HEAD_DIM
        kvc_ref[0, 0, g] = pc[:, base + lo:base + lo + HEAD_DIM].astype(BF16)
        kvc_ref[0, 1, g] = pc[:, base + KV_WIDTH + lo:base + KV_WIDTH + lo + HEAD_DIM].astype(BF16)
        ksa_ref[0, g, :, 0:SEL_LANES] = sel_rows
        ksa_ref[0, g, :, SEL_LANES:SEL_LANES + HEAD_DIM] = (
            pc[:, base + 2 * KV_WIDTH + lo:base + 2 * KV_WIDTH + lo + HEAD_DIM].astype(BF16))
        vs_ref[0, g, :, 0:HEAD_DIM] = (
            pc[:, base + 3 * KV_WIDTH + lo:base + 3 * KV_WIDTH + lo + HEAD_DIM].astype(BF16))
        vs_ref[0, g, :, HEAD_DIM:2 * HEAD_DIM] = ones_col
        kw_ref[0, g] = pc[:, base + 4 * KV_WIDTH + lo:base + 4 * KV_WIDTH + lo + HEAD_DIM].astype(BF16)
        vw_ref[0, g, :, 0:HEAD_DIM] = (
            pc[:, base + 5 * KV_WIDTH + lo:base + 5 * KV_WIDTH + lo + HEAD_DIM].astype(BF16))
        vw_ref[0, g, :, HEAD_DIM:2 * HEAD_DIM] = ones_col


def _proj_mix(x, prev, alpha, mem, w_in_p, sln_g, sln_b, sw, sb, cw, cb, wbd, ba, bx, lam, wmem, wout_abd):
    bsz, seq, dm = x.shape
    ts = TS_PROJ
    n_mem = mem.shape[1]
    tok = lambda b, s: (b, s, 0)

    def whole(a):
        return pl.BlockSpec(a.shape, lambda b, s: (0,) * a.ndim, pipeline_mode=pl.Buffered(1))

    consts = (w_in_p, sln_g, sln_b, sw, sb, cw, cb, wbd, ba, bx, lam, wmem, wout_abd)
    args = [x]
    in_specs = [pl.BlockSpec((1, ts, dm), tok)]
    if prev is not None:
        part_p, yc_p, wout_c_p, ln_g_p, ln_b_p = prev
        args += [part_p, yc_p, wout_c_p, ln_g_p, ln_b_p]
        in_specs += [pl.BlockSpec((1, ts, dm), tok), pl.BlockSpec((1, ts, NSA_WIDTH), tok),
                     whole(wout_c_p), whole(ln_g_p), whole(ln_b_p)]
    args += [mem, *consts]
    in_specs += [pl.BlockSpec((1, n_mem, dm), lambda b, s: (b, 0, 0), pipeline_mode=pl.Buffered(1))]
    in_specs += [whole(a) for a in consts]
    kv_spec = pl.BlockSpec((1, NSA_KV, ts, HEAD_DIM), lambda b, s: (b, 0, s, 0))
    va_spec = pl.BlockSpec((1, NSA_KV, ts, 2 * HEAD_DIM), lambda b, s: (b, 0, s, 0))
    out_specs = [
        pl.BlockSpec((1, ts, dm), lambda b, s: (b, s, 0)),
        pl.BlockSpec((1, ts, NSA_WIDTH), lambda b, s: (b, s, 0)),
        pl.BlockSpec((1, ts, NSA_WIDTH), lambda b, s: (b, s, 0)),
        pl.BlockSpec((1, ts, NSA_KV * LANES), lambda b, s: (b, s, 0)),
        pl.BlockSpec((1, 2, NSA_KV, ts, HEAD_DIM), lambda b, s: (b, 0, 0, s, 0)),
        pl.BlockSpec((1, NSA_KV, ts, SEL_LANES + HEAD_DIM), lambda b, s: (b, 0, s, 0)),
        va_spec, kv_spec, va_spec,
    ]
    out_shape = [
        jax.ShapeDtypeStruct((bsz, seq, dm), F32),
        jax.ShapeDtypeStruct((bsz, seq, NSA_WIDTH), BF16),
        jax.ShapeDtypeStruct((bsz, seq, NSA_WIDTH), F32),
        jax.ShapeDtypeStruct((bsz, seq, NSA_KV * LANES), F32),
        jax.ShapeDtypeStruct((bsz, 2, NSA_KV, seq, HEAD_DIM), BF16),
        jax.ShapeDtypeStruct((bsz, NSA_KV, seq, SEL_LANES + HEAD_DIM), BF16),
        jax.ShapeDtypeStruct((bsz, NSA_KV, seq, 2 * HEAD_DIM), BF16),
        jax.ShapeDtypeStruct((bsz, NSA_KV, seq, HEAD_DIM), BF16),
        jax.ShapeDtypeStruct((bsz, NSA_KV, seq, 2 * HEAD_DIM), BF16),
    ]
    if prev is not None:
        out_specs = [pl.BlockSpec((1, ts, dm), tok)] + out_specs
        out_shape = [jax.ShapeDtypeStruct((bsz, seq, dm), F32)] + out_shape
    return pl.pallas_call(
        functools.partial(_proj_mix_kernel, alpha=alpha if prev is not None else None),
        grid=(bsz, seq // ts),
        in_specs=in_specs,
        out_specs=out_specs,
        out_shape=out_shape,
        scratch_shapes=[
            pltpu.VMEM((ts + 8, LRU_WIDTH), F32),
            pltpu.VMEM((1, LRU_WIDTH), F32),
            pltpu.VMEM((n_mem, MEM_WIDTH), BF16),
            pltpu.VMEM((n_mem, MEM_WIDTH), BF16),
        ],
        compiler_params=pltpu.CompilerParams(
            dimension_semantics=("arbitrary", "arbitrary"), vmem_limit_bytes=VMEM_LIMIT),
        name="proj_mix",
    )(*args)


def _compress_kernel(fc_ref, pos_ref, w1_ref, w2_ref, out_ref):
    fc = fc_ref[0, 0, 0].astype(F32)
    half = (CMP_LEN // 2) * HEAD_DIM
    pos = pos_ref[0]
    fa = (fc + pos[:, :half]).astype(BF16)
    fb = (fc + pos[:, half:]).astype(BF16)
    ha = _dot(fa, w1_ref[0, :half, :])
    hb = _dot(fb, w1_ref[0, half:, :])
    hb_next = jnp.concatenate([hb[1:], hb[:1]], axis=0)
    hid = _gelu(ha + hb_next)
    out_ref[0, 0, 0] = _dot(hid.astype(BF16), w2_ref[0]).astype(BF16)


def _compress(fc, pos_flat, w1, w2):
    bsz, _, _, n_chunk, width = fc.shape
    return pl.pallas_call(
        _compress_kernel,
        grid=(bsz, 2, NSA_KV),
        in_specs=[
            pl.BlockSpec((1, 1, 1, n_chunk, width), lambda b, k, g: (b, k, g, 0, 0)),
            pl.BlockSpec((1, 1, CMP_LEN * HEAD_DIM), lambda b, k, g: (k, 0, 0)),
            pl.BlockSpec((1, CMP_LEN * HEAD_DIM, CMP_HIDDEN), lambda b, k, g: (k, 0, 0)),
            pl.BlockSpec((1, CMP_HIDDEN, HEAD_DIM), lambda b, k, g: (k, 0, 0)),
        ],
        out_specs=pl.BlockSpec((1, 1, 1, n_chunk, HEAD_DIM), lambda b, k, g: (b, k, g, 0, 0)),
        out_shape=jax.ShapeDtypeStruct((bsz, 2, NSA_KV, n_chunk, HEAD_DIM), BF16),
        compiler_params=pltpu.CompilerParams(
            dimension_semantics=("arbitrary", "arbitrary", "arbitrary"), vmem_limit_bytes=VMEM_LIMIT),
        name="compress",
    )(fc, pos_flat, w1, w2)


def _nsa_kernel(q_ref, zc_ref, gl_ref, kcv_ref, ksa_ref, vs_ref, kw_ref, vw_ref, ovt_ref, bias_ref, y_ref, s_buf,
                m_ref, acc_ref, *, n_sel):
    qi = pl.program_id(2)
    tq = q_ref.shape[1]
    rows = NSA_GROUP * tq
    s0 = qi * tq
    q = q_ref[0]
    q_heads = [q[:, h * HEAD_DIM:(h + 1) * HEAD_DIM] for h in range(NSA_GROUP)]
    qs = jnp.concatenate(q_heads, axis=0)
    t_q = s0 + lax.broadcasted_iota(jnp.int32, (tq, 1), 0)

    blk = lax.broadcasted_iota(jnp.int32, (SEL_LANES, 1), 0)
    blk_f = blk.astype(F32)

    def compressed_branch():
        kc = kcv_ref[0, 0, 0]
        vc = kcv_ref[0, 1, 0]
        n_cmp = kc.shape[0]
        cmp_end = lax.broadcasted_iota(jnp.int32, (1, n_cmp), 1) * CMP_STRIDE + (CMP_LEN - 1)
        valid_c = cmp_end <= t_q
        any_valid = t_q >= CMP_LEN - 1
        o_c = []
        ps = None
        for h in range(NSA_GROUP):
            sc = jnp.where(valid_c, _dot_nt(q_heads[h], kc), NEG)
            m = jnp.max(sc, axis=-1, keepdims=True)
            p = jnp.exp2(sc - m)
            l = jnp.sum(p, axis=-1, keepdims=True)
            pn = p * jnp.where(any_valid, 1.0 / l, 0.0)
            o_c.append(_dot(pn.astype(BF16), vc))
            ps = pn if ps is None else ps + pn
        ps_hi = ps.astype(BF16)
        ps_lo = (ps - ps_hi.astype(F32)).astype(BF16)
        imp = _dot_nt(ovt_ref[...], ps_hi) + _dot_nt(ovt_ref[...], ps_lo)
        cur = (s0 + lax.broadcasted_iota(jnp.int32, (1, tq), 1)) // SLC_BLOCK
        future = blk > cur
        forced = (blk == 0) | (blk == cur) | (blk == cur - 1)
        return jnp.where(future, -1.0, imp + jnp.where(forced, FORCE_BONUS, 0.0)), tuple(o_c)

    def knock_out(work_i, rounds):
        for _ in range(rounds):
            mx = jnp.max(work_i, axis=0, keepdims=True)
            idx = jnp.min(jnp.where(work_i == mx, blk_f, float(SEL_LANES)), axis=0, keepdims=True)
            work_i = jnp.where(blk_f == idx, -jnp.inf, work_i)
        return work_i

    n_tiles = ksa_ref.shape[2] // TK

    def tile_start(kt):
        return pl.multiple_of(jnp.clip(kt, 0, n_tiles - 1) * TK, TK)

    one = 1 + jnp.minimum(qi, 0)

    def run_tiles(q_op, k_ref, v_ref, kt0, stride, n_plain, n_masked, bias_fn, side_fn=None, side=None):
        def tile(ref, i):
            return ref[0, 0, pl.ds(tile_start(kt0 + stride * i), TK), :]

        def step(i, slot, state, prefetch, masked, pos=0):
            m_i, acc, side_i = state
            s = s_buf[slot]
            if prefetch:
                s_buf[1 - slot] = _dot_nt(q_op, tile(k_ref, i + 1))
            if masked:
                s = (s.reshape(NSA_GROUP, tq, TK) + bias_fn(pos)[None]).reshape(rows, TK)
            m_new = jnp.maximum(m_i, jnp.broadcast_to(jnp.max(s, axis=-1, keepdims=True), m_i.shape))
            p_t = jnp.exp2(s - jnp.concatenate([m_new] * (TK // LANES), axis=1)).astype(BF16)
            acc = jnp.exp2(m_i - m_new) * acc + _dot(p_t, tile(v_ref, i))
            return m_new, acc, (side_fn(pos, side_i) if side_fn is not None else side_i)

        def group(width, first, masked, prefetch_last=True, slot0=0):
            def body(j, side_i):
                state = (m_ref[...], acc_ref[...], side_i)
                for i in range(width):
                    slot = slot0 if width == 1 else i % 2
                    state = step(first + width * j + i, slot, state, prefetch_last or i + 1 < width, masked, i)
                m_ref[...] = state[0]
                acc_ref[...] = state[1]
                return state[2]
            return body

        s_buf[0] = _dot_nt(q_op, tile(k_ref, 0))
        m_ref[...] = jnp.full(m_ref.shape, NEG, F32)
        acc_ref[...] = jnp.zeros(acc_ref.shape, F32)
        if isinstance(n_plain, int):
            assert n_plain == 0 and n_masked == 2
            side = lax.fori_loop(0, one, group(2, 0, True, prefetch_last=False), side)
        else:
            assert n_masked == 1
            n4 = n_plain // 4
            side = lax.fori_loop(0, n4, group(4, 0, False), side)
            side = lax.fori_loop(0, (n_plain % 4) // 2, group(2, 4 * n4, False), side)
            side = lax.fori_loop(0, n_plain % 2, group(1, n_plain - 1, False), side)
            side = lax.fori_loop(0, one, group(1, n_plain, True, prefetch_last=False, slot0=n_plain % 2), side)
        acc = acc_ref[...]
        return acc[:, :HEAD_DIM] * (1.0 / acc[:, HEAD_DIM:HEAD_DIM + 1]), side

    assert tq == TK == WINDOW
    older = jnp.where(qi > 0, bias_ref[1], NEG)

    assert n_sel % 2 == 0
    work, o_c = compressed_branch()
    o_w, work = run_tiles(qs, kw_ref, vw_ref, qi, -1, 0, 2, lambda i: bias_ref[0] if i == 0 else older,
                          side_fn=lambda pos, w: knock_out(w, n_sel // 2), side=work)
    sel_t = jnp.where(work == -jnp.inf, 0.0, -1.0)
    sel_rows = jnp.concatenate([sel_t.T.astype(BF16)] * NSA_GROUP, axis=0)
    q_aug = jnp.concatenate([sel_rows, qs], axis=1)

    o_s, _ = run_tiles(q_aug, ksa_ref, vs_ref, 0, 1, qi, 1, lambda i: bias_ref[0])

    gates = _sigmoid(gl_ref[0])
    heads = []
    for h in range(NSA_GROUP):
        y_h = o_c[h] * gates[:, 3 * h:3 * h + 1]
        y_h = y_h + o_s[h * tq:(h + 1) * tq] * gates[:, 3 * h + 1:3 * h + 2]
        y_h = y_h + o_w[h * tq:(h + 1) * tq] * gates[:, 3 * h + 2:3 * h + 3]
        heads.append(y_h)
    y = jnp.concatenate(heads, axis=1)
    y_ref[0] = (y * _silu(zc_ref[0])).astype(BF16)


def _nsa(q, zc, gl, kcv, ksa, vs, kw, vw, overlap, tile_bias):
    bsz, seq, _ = q.shape
    n_cmp = kcv.shape[3]
    gw = NSA_GROUP * HEAD_DIM
    n_sel = min(N_SELECT, seq // SLC_BLOCK)
    per_bg = lambda b, g, i: (b, g, 0, 0)
    return pl.pallas_call(
        functools.partial(_nsa_kernel, n_sel=n_sel),
        grid=(bsz, NSA_KV, seq // TQ),
        in_specs=[
            pl.BlockSpec((1, TQ, gw), lambda b, g, i: (b, i, g)),
            pl.BlockSpec((1, TQ, gw), lambda b, g, i: (b, i, g)),
            pl.BlockSpec((1, TQ, LANES), lambda b, g, i: (b, i, g)),
            pl.BlockSpec((1, 2, 1, n_cmp, HEAD_DIM), lambda b, g, i: (b, 0, g, 0, 0)),
            pl.BlockSpec((1, 1, seq, SEL_LANES + HEAD_DIM), per_bg, pipeline_mode=pl.Buffered(1)),
            pl.BlockSpec((1, 1, seq, 2 * HEAD_DIM), per_bg, pipeline_mode=pl.Buffered(1)),
            pl.BlockSpec((1, 1, seq, HEAD_DIM), per_bg, pipeline_mode=pl.Buffered(1)),
            pl.BlockSpec((1, 1, seq, 2 * HEAD_DIM), per_bg, pipeline_mode=pl.Buffered(1)),
            pl.BlockSpec(overlap.shape, lambda b, g, i: (0, 0), pipeline_mode=pl.Buffered(1)),
            pl.BlockSpec(tile_bias.shape, lambda b, g, i: (0, 0, 0), pipeline_mode=pl.Buffered(1)),
        ],
        out_specs=pl.BlockSpec((1, TQ, gw), lambda b, g, i: (b, i, g)),
        out_shape=jax.ShapeDtypeStruct((bsz, seq, NSA_WIDTH), BF16),
        scratch_shapes=[pltpu.VMEM((2, NSA_GROUP * TQ, TK), F32),
                        pltpu.VMEM((NSA_GROUP * TQ, LANES), F32),
                        pltpu.VMEM((NSA_GROUP * TQ, 2 * HEAD_DIM), F32)],
        compiler_params=pltpu.CompilerParams(
            dimension_semantics=("arbitrary", "arbitrary", "arbitrary"), vmem_limit_bytes=VMEM_LIMIT),
        name="nsa",
    )(q, zc, gl, kcv, ksa, vs, kw, vw, overlap, tile_bias)


def _final_kernel(x_ref, part_ref, yc_ref, wout_ref, g_ref, b_ref, o_ref, *, alpha):
    o_ref[0] = _layer_output(x_ref[0], part_ref[0], yc_ref[0], wout_ref[...], g_ref[...], b_ref[...], alpha)


def _final(x, part, yc, wout_c, ln_g, ln_b, alpha):
    bsz, seq, dm = x.shape
    ts = TS_PROJ
    tok = lambda b, s: (b, s, 0)
    return pl.pallas_call(
        functools.partial(_final_kernel, alpha=alpha),
        grid=(bsz, seq // ts),
        in_specs=[
            pl.BlockSpec((1, ts, dm), tok),
            pl.BlockSpec((1, ts, dm), tok),
            pl.BlockSpec((1, ts, NSA_WIDTH), tok),
            pl.BlockSpec(wout_c.shape, lambda b, s: (0, 0)),
            pl.BlockSpec(ln_g.shape, lambda b, s: (0, 0)),
            pl.BlockSpec(ln_b.shape, lambda b, s: (0, 0)),
        ],
        out_specs=pl.BlockSpec((1, ts, dm), tok),
        out_shape=jax.ShapeDtypeStruct((bsz, seq, dm), F32),
        compiler_params=pltpu.CompilerParams(
            dimension_semantics=("arbitrary", "arbitrary"), vmem_limit_bytes=VMEM_LIMIT),
        name="final",
    )(x, part, yc, wout_c, ln_g, ln_b)


def _pack_w_in(w_in):
    depth, dm, _ = w_in.shape
    gate0 = COL_C + 2 * NSA_WIDTH + 6 * KV_WIDTH
    per_group = N_GATES // NSA_KV
    gates = w_in[:, :, gate0:gate0 + N_GATES]
    gate_blocks = []
    for g in range(NSA_KV):
        blk = gates[:, :, g * per_group:(g + 1) * per_group]
        gate_blocks.append(jnp.pad(blk, ((0, 0), (0, 0), (0, LANES - per_group))))
    d_cols = w_in[:, :, gate0 + N_GATES:]
    return jnp.concatenate([w_in[:, :, :gate0]] + gate_blocks + [d_cols], axis=-1).astype(BF16)


def _block_diag_gates(wa, wx):
    depth = wa.shape[0]
    per_half = LRU_BLOCKS // 2
    half = per_half * LRU_BLOCK
    halves = []
    for hf in range(2):
        mats = []
        for w in (wa, wx):
            m = jnp.zeros((depth, half, half), F32)
            for j in range(per_half):
                m = m.at[:, j * LRU_BLOCK:(j + 1) * LRU_BLOCK, j * LRU_BLOCK:(j + 1) * LRU_BLOCK].set(
                    w[:, hf * per_half + j])
            mats.append(m)
        halves.append(jnp.concatenate(mats, axis=-1))
    return jnp.stack(halves, axis=1).astype(BF16)


def _overlap_matrix(n_cmp_pad, n_cmp, n_slc):
    ci = np.arange(n_cmp_pad)[:, None] * CMP_STRIDE
    sj = np.arange(SEL_LANES)[None, :] * SLC_BLOCK
    ov = (ci < sj + SLC_BLOCK) & (ci + CMP_LEN > sj)
    ov &= (np.arange(n_cmp_pad)[:, None] < n_cmp) & (np.arange(SEL_LANES)[None, :] < n_slc)
    return jnp.asarray(ov.T.astype(np.float32), dtype=BF16)


def _tile_bias():
    r = np.arange(TQ)[:, None]
    k = np.arange(TK)[None, :]
    diag = np.where(k <= r, 0.0, NEG)
    older = np.where(k > r, 0.0, NEG)
    return jnp.asarray(np.stack([diag, older]), dtype=F32)


def kernel(x, mem, w_in, sgu_ln_g, sgu_ln_b, sgu_w, sgu_b, conv_w, conv_b, lru_wa, lru_ba, lru_wx, lru_bx,
           lru_lambda, cmp_pos, cmp_w1, cmp_w2, w_mem_kv, w_out, ln_g, ln_b):
    depth = w_in.shape[0]
    bsz, seq, dm = x.shape
    assert TQ == TK and WINDOW <= TK
    assert seq % TK == 0 and seq // SLC_BLOCK <= SEL_LANES
    alpha = (2 * depth) ** 0.25
    n_chunk = seq // CMP_STRIDE
    n_cmp = (seq - CMP_LEN) // CMP_STRIDE + 1

    w_in_p = _pack_w_in(w_in)
    wbd = _block_diag_gates(lru_wa, lru_wx)
    wmem = w_mem_kv.astype(BF16)
    c0 = A_WIDTH + LRU_WIDTH
    wout_abd = jnp.concatenate([w_out[:, :c0], w_out[:, c0 + NSA_WIDTH:]], axis=1).astype(BF16)
    wout_c = w_out[:, c0:c0 + NSA_WIDTH].astype(BF16)
    w1 = cmp_w1.astype(BF16)
    w2 = cmp_w2.astype(BF16)
    pos_flat = cmp_pos.reshape(depth, 2, 1, CMP_LEN * HEAD_DIM)
    overlap = _overlap_matrix(n_chunk, n_cmp, seq // SLC_BLOCK)
    tile_bias = _tile_bias()
    row = lambda a: a[:, None, :]

    prev = None
    for l in range(depth):
        outs = _proj_mix(
            x, prev, alpha, mem, w_in_p[l], row(sgu_ln_g)[l], row(sgu_ln_b)[l], sgu_w[l], sgu_b[l][:, :, None],
            conv_w[l], row(conv_b)[l], wbd[l], row(lru_ba)[l], row(lru_bx)[l], row(lru_lambda)[l],
            wmem[l], wout_abd[l])
        if prev is not None:
            x, outs = outs[0], outs[1:]
        part, q, zc, gl, kvc_raw, ksa, vs, kw, vw = outs
        fc = kvc_raw.reshape(bsz, 2, NSA_KV, n_chunk, CMP_STRIDE * HEAD_DIM)
        kcv = _compress(fc, pos_flat[l], w1[l], w2[l])
        yc = _nsa(q, zc, gl, kcv, ksa, vs, kw, vw, overlap, tile_bias)
        prev = (part, yc, wout_c[l], row(ln_g)[l], row(ln_b)[l])
    return _final(x, *prev, alpha)
```

```python
import functools

import jax
import jax.numpy as jnp
import numpy as np
from jax import lax
from jax.experimental import pallas as pl
from jax.experimental.pallas import tpu as pltpu

F32 = jnp.float32
BF16 = jnp.bfloat16

CHUNK = 128
A_GROUPS = 4
A_WIDTH = 512
LRU_WIDTH = 512
LRU_BLOCKS = 8
LRU_BLOCK = 64
CONV_W = 4
C_LRU = 8.0
HEAD_DIM = 64
NSA_KV = 2
NSA_GROUP = 4
NSA_WIDTH = 512
KV_WIDTH = 128
CMP_LEN = 32
CMP_STRIDE = 16
CMP_HIDDEN = 128
SLC_BLOCK = 64
N_SELECT = 16
WINDOW = 512
N_GATES = 24
MEM_HEADS = 4
MEM_WIDTH = 256
LN_EPS = 1e-5
NEG = -1e30
FORCE_BONUS = 1e4

LANES = 128
SUBLANES = 8
SEL_LANES = 128
VMEM_LIMIT = 56 * 1024 * 1024

COL_A = 0
COL_B = 1536
COL_C = 2560
COL_TAIL = 4352
TAIL_G = 0
TAIL_D = 256

TS_PROJ = 512
TQ = 512
TK = 512

Q_SCALE = HEAD_DIM ** -0.5 * float(np.log2(np.e))


def _sigmoid(x):
    return 1.0 / (1.0 + jnp.exp(-x))


def _silu(x):
    return x * _sigmoid(x)


def _gelu(x):
    return jax.nn.gelu(x, approximate=True)


def _dot(a, b):
    return jnp.dot(a, b, preferred_element_type=F32)


def _dot_nt(a, b):
    return lax.dot_general(a, b, (((1,), (1,)), ((), ())), preferred_element_type=F32)


def _layer_output(x, part, yc, wout_c, g, b, alpha):
    y = alpha * x + (part + _dot(yc, wout_c))
    mu = jnp.mean(y, axis=-1, keepdims=True)
    var = jnp.mean(jnp.square(y - mu), axis=-1, keepdims=True)
    return ((y - mu) * lax.rsqrt(var + LN_EPS)) * g + b


def _proj_mix_kernel(*refs, alpha):
    if alpha is not None:
        x_ref, part_p_ref, yc_p_ref, woutc_p_ref, lng_p_ref, lnb_p_ref = refs[:6]
        refs = refs[6:]
    else:
        x_ref, refs = refs[0], refs[1:]
    (mem_ref, w_in_ref, w_tail_ref, sln_g_ref, sln_b_ref, sw_ref, sb_ref, cw_ref, cb_ref, wbd_ref, ba_ref, bx_ref,
     lam_ref, wmem_ref, wout_ref) = refs[:15]
    refs = refs[15:]
    if alpha is not None:
        xo_ref, refs = refs[0], refs[1:]
    (part_ref, q_ref, zc_ref, gl_ref, kvc_ref, ksa_ref, vs_ref, kw_ref, vw_ref,
     conv_buf, h_carry, memk, memv) = refs
    s = pl.program_id(1)
    ts = x_ref.shape[1]
    if alpha is not None:
        x_new = _layer_output(x_ref[0], part_p_ref[0], yc_p_ref[0], woutc_p_ref[...], lng_p_ref[...],
                              lnb_p_ref[...], alpha)
        xo_ref[0] = x_new
        xb = x_new.astype(BF16)
    else:
        xb = x_ref[0].astype(BF16)

    @pl.when(s == 0)
    def _():
        conv_buf[0:8, :] = jnp.zeros((8, LRU_WIDTH), F32)
        h_carry[...] = jnp.zeros_like(h_carry)
        kv = _dot(mem_ref[0].astype(BF16), wmem_ref[...])
        memk[...] = kv[:, :MEM_WIDTH].astype(BF16)
        memv[...] = kv[:, MEM_WIDTH:].astype(BF16)

    pa = _dot(xb, w_in_ref[:, COL_A:COL_A + 3 * A_WIDTH])
    u = _gelu(pa[:, 0:A_WIDTH])
    v = _gelu(pa[:, A_WIDTH:2 * A_WIDTH])
    za = pa[:, 2 * A_WIDTH:3 * A_WIDTH]
    row = lax.broadcasted_iota(jnp.int32, (CHUNK, CHUNK), 0)
    col = lax.broadcasted_iota(jnp.int32, (CHUNK, CHUNK), 1)
    causal = col <= row
    sv_groups = []
    for g in range(A_GROUPS):
        vg = v[:, g * CHUNK:(g + 1) * CHUNK]
        mu = jnp.mean(vg, axis=-1, keepdims=True)
        var = jnp.mean(jnp.square(vg - mu), axis=-1, keepdims=True)
        vn = (vg - mu) * lax.rsqrt(var + LN_EPS)
        vn = vn * sln_g_ref[:, g * CHUNK:(g + 1) * CHUNK] + sln_b_ref[:, g * CHUNK:(g + 1) * CHUNK]
        vnb = vn.astype(BF16)
        wg = jnp.where(causal, sw_ref[g], 0.0).astype(BF16)
        bias = sb_ref[g]
        chunks = []
        for c in range(ts // CHUNK):
            chunks.append(_dot(wg, vnb[c * CHUNK:(c + 1) * CHUNK]) + bias)
        sv_groups.append(jnp.concatenate(chunks, axis=0))
    sv = jnp.concatenate(sv_groups, axis=1)
    ya = (u * sv) * _silu(za)

    pb = _dot(xb, w_in_ref[:, COL_B:COL_B + 2 * LRU_WIDTH])
    xbv = pb[:, :LRU_WIDTH]
    zb = pb[:, LRU_WIDTH:]
    conv_buf[8:8 + ts, :] = xbv
    xc = cb_ref[...] + conv_buf[5:5 + ts, :] * cw_ref[0:1, :]
    xc = xc + conv_buf[6:6 + ts, :] * cw_ref[1:2, :]
    xc = xc + conv_buf[7:7 + ts, :] * cw_ref[2:3, :]
    xc = xc + xbv * cw_ref[3:4, :]
    conv_buf[0:8, :] = xbv[ts - 8:ts, :]
    xcb = xc.astype(BF16)
    half = LRU_WIDTH // 2
    g0 = _dot(xcb[:, :half], wbd_ref[0])
    g1 = _dot(xcb[:, half:], wbd_ref[1])
    r = _sigmoid(jnp.concatenate([g0[:, :half], g1[:, :half]], axis=1) + ba_ref[...])
    ig = _sigmoid(jnp.concatenate([g0[:, half:], g1[:, half:]], axis=1) + bx_ref[...])
    nlam = -lam_ref[...]
    softplus = jnp.maximum(nlam, 0.0) + jnp.log1p(jnp.exp(-jnp.abs(nlam)))
    log_a = (-C_LRU * r) * softplus
    a_cum = jnp.exp(log_a)
    th = jnp.tanh(log_a)
    b_cum = jnp.sqrt((-2.0 * th) / (1.0 - th)) * (ig * xc)
    sub = SUBLANES
    a_cum = a_cum.reshape(ts // sub, sub, LRU_WIDTH)
    b_cum = b_cum.reshape(ts // sub, sub, LRU_WIDTH)
    row_in_block = lax.broadcasted_iota(jnp.int32, (1, sub, 1), 1)
    d = 1
    while d < sub:
        keep = row_in_block >= d
        a_sh = jnp.where(keep, pltpu.roll(a_cum, d, axis=1), 1.0)
        b_sh = jnp.where(keep, pltpu.roll(b_cum, d, axis=1), 0.0)
        b_cum = b_cum + a_cum * b_sh
        a_cum = a_cum * a_sh
        d *= 2
    carry = h_carry[...]
    blocks = []
    for blk in range(ts // sub):
        hb = b_cum[blk] + a_cum[blk] * carry
        carry = hb[sub - 1:sub, :]
        blocks.append(hb)
    h = jnp.concatenate(blocks, axis=0)
    h_carry[...] = carry
    yb = h * _silu(zb)

    pd = _dot(xb, w_tail_ref[:, TAIL_D:TAIL_D + 2 * MEM_WIDTH])
    qd = pd[:, :MEM_WIDTH] * (HEAD_DIM ** -0.5)
    zd = pd[:, MEM_WIDTH:]
    head_of_lane = lax.broadcasted_iota(jnp.int32, (1, MEM_WIDTH), 1) // HEAD_DIM
    od = jnp.zeros((ts, MEM_WIDTH), F32)
    for hd in range(MEM_HEADS):
        hm = head_of_lane == hd
        qh = jnp.where(hm, qd, 0.0).astype(BF16)
        sc = _dot_nt(qh, memk[...])
        m = jnp.max(sc, axis=-1, keepdims=True)
        p = jnp.exp(sc - m)
        l = jnp.sum(p, axis=-1, keepdims=True)
        pv = _dot(p.astype(BF16), memv[...])
        od = od + jnp.where(hm, pv * (1.0 / l), 0.0)
    yd = od * _silu(zd)

    y_abd = jnp.concatenate([ya, yb, yd], axis=1).astype(BF16)
    part_ref[0] = _dot(y_abd, wout_ref[...])

    pc = _dot(xb, w_in_ref[:, COL_C:COL_C + 2 * NSA_WIDTH + 6 * KV_WIDTH])
    q_ref[0] = (pc[:, :NSA_WIDTH] * Q_SCALE).astype(BF16)
    zc_ref[0] = pc[:, NSA_WIDTH:2 * NSA_WIDTH]
    gl_ref[0] = _dot(xb, w_tail_ref[:, TAIL_G:TAIL_G + NSA_KV * LANES])
    base = 2 * NSA_WIDTH
    kpos = s * ts + lax.broadcasted_iota(jnp.int32, (ts, SEL_LANES), 0)
    blk = lax.broadcasted_iota(jnp.int32, (ts, SEL_LANES), 1)
    sel_rows = jnp.where((kpos // SLC_BLOCK) == blk, -NEG, 0.0).astype(BF16)
    ones_col = jnp.where(lax.broadcasted_iota(jnp.int32, (ts, HEAD_DIM), 1) == 0, 1.0, 0.0).astype(BF16)
    for g in range(NSA_KV):
        lo = g * HEAD_DIM
        kvc_ref[0, 0, g] = pc[:, base + lo:base + lo + HEAD_DIM].astype(BF16)
        kvc_ref[0, 1, g] = pc[:, base + KV_WIDTH + lo:base + KV_WIDTH + lo + HEAD_DIM].astype(BF16)
        ksa_ref[0, g, :, 0:SEL_LANES] = sel_rows
        ksa_ref[0, g, :, SEL_LANES:SEL_LANES + HEAD_DIM] = (
            pc[:, base + 2 * KV_WIDTH + lo:base + 2 * KV_WIDTH + lo + HEAD_DIM].astype(BF16))
        vs_ref[0, g, :, 0:HEAD_DIM] = (
            pc[:, base + 3 * KV_WIDTH + lo:base + 3 * KV_WIDTH + lo + HEAD_DIM].astype(BF16))
        vs_ref[0, g, :, HEAD_DIM:2 * HEAD_DIM] = ones_col
        kw_ref[0, g] = pc[:, base + 4 * KV_WIDTH + lo:base + 4 * KV_WIDTH + lo + HEAD_DIM].astype(BF16)
        vw_ref[0, g, :, 0:HEAD_DIM] = (
            pc[:, base + 5 * KV_WIDTH + lo:base + 5 * KV_WIDTH + lo + HEAD_DIM].astype(BF16))
        vw_ref[0, g, :, HEAD_DIM:2 * HEAD_DIM] = ones_col


def _proj_mix(x, prev, alpha, mem, w_in_b, w_tail, sln_g, sln_b, sw, sb, cw, cb, wbd, ba, bx, lam, wmem,
              wout_abd):
    bsz, seq, dm = x.shape
    ts = TS_PROJ
    n_mem = mem.shape[1]
    tok = lambda b, s: (b, s, 0)

    def whole(a):
        return pl.BlockSpec(a.shape, lambda b, s: (0,) * a.ndim, pipeline_mode=pl.Buffered(1))

    consts = (w_in_b, w_tail, sln_g, sln_b, sw, sb, cw, cb, wbd, ba, bx, lam, wmem, wout_abd)
    args = [x]
    in_specs = [pl.BlockSpec((1, ts, dm), tok)]
    if prev is not None:
        part_p, yc_p, wout_c_p, ln_g_p, ln_b_p = prev
        args += [part_p, yc_p, wout_c_p, ln_g_p, ln_b_p]
        in_specs += [pl.BlockSpec((1, ts, dm), tok), pl.BlockSpec((1, ts, NSA_WIDTH), tok),
                     whole(wout_c_p), whole(ln_g_p), whole(ln_b_p)]
    args += [mem, *consts]
    in_specs += [pl.BlockSpec((1, n_mem, dm), lambda b, s: (b, 0, 0), pipeline_mode=pl.Buffered(1))]
    in_specs += [whole(a) for a in consts]
    kv_spec = pl.BlockSpec((1, NSA_KV, ts, HEAD_DIM), lambda b, s: (b, 0, s, 0))
    va_spec = pl.BlockSpec((1, NSA_KV, ts, 2 * HEAD_DIM), lambda b, s: (b, 0, s, 0))
    out_specs = [
        pl.BlockSpec((1, ts, dm), lambda b, s: (b, s, 0)),
        pl.BlockSpec((1, ts, NSA_WIDTH), lambda b, s: (b, s, 0)),
        pl.BlockSpec((1, ts, NSA_WIDTH), lambda b, s: (b, s, 0)),
        pl.BlockSpec((1, ts, NSA_KV * LANES), lambda b, s: (b, s, 0)),
        pl.BlockSpec((1, 2, NSA_KV, ts, HEAD_DIM), lambda b, s: (b, 0, 0, s, 0)),
        pl.BlockSpec((1, NSA_KV, ts, SEL_LANES + HEAD_DIM), lambda b, s: (b, 0, s, 0)),
        va_spec, kv_spec, va_spec,
    ]
    out_shape = [
        jax.ShapeDtypeStruct((bsz, seq, dm), F32),
        jax.ShapeDtypeStruct((bsz, seq, NSA_WIDTH), BF16),
        jax.ShapeDtypeStruct((bsz, seq, NSA_WIDTH), F32),
        jax.ShapeDtypeStruct((bsz, seq, NSA_KV * LANES), F32),
        jax.ShapeDtypeStruct((bsz, 2, NSA_KV, seq, HEAD_DIM), BF16),
        jax.ShapeDtypeStruct((bsz, NSA_KV, seq, SEL_LANES + HEAD_DIM), BF16),
        jax.ShapeDtypeStruct((bsz, NSA_KV, seq, 2 * HEAD_DIM), BF16),
        jax.ShapeDtypeStruct((bsz, NSA_KV, seq, HEAD_DIM), BF16),
        jax.ShapeDtypeStruct((bsz, NSA_KV, seq, 2 * HEAD_DIM), BF16),
    ]
    if prev is not None:
        out_specs = [pl.BlockSpec((1, ts, dm), tok)] + out_specs
        out_shape = [jax.ShapeDtypeStruct((bsz, seq, dm), F32)] + out_shape
    return pl.pallas_call(
        functools.partial(_proj_mix_kernel, alpha=alpha if prev is not None else None),
        grid=(bsz, seq // ts),
        in_specs=in_specs,
        out_specs=out_specs,
        out_shape=out_shape,
        scratch_shapes=[
            pltpu.VMEM((ts + 8, LRU_WIDTH), F32),
            pltpu.VMEM((1, LRU_WIDTH), F32),
            pltpu.VMEM((n_mem, MEM_WIDTH), BF16),
            pltpu.VMEM((n_mem, MEM_WIDTH), BF16),
        ],
        compiler_params=pltpu.CompilerParams(
            dimension_semantics=("arbitrary", "arbitrary"), vmem_limit_bytes=VMEM_LIMIT),
        name="proj_mix",
    )(*args)


def _compress_kernel(fc_ref, pos_ref, w1_ref, w2_ref, out_ref):
    fc = fc_ref[0, 0, 0].astype(F32)
    half = (CMP_LEN // 2) * HEAD_DIM
    pos = pos_ref[0]
    fa = (fc + pos[:, :half]).astype(BF16)
    fb = (fc + pos[:, half:]).astype(BF16)
    ha = _dot(fa, w1_ref[0, :half, :])
    hb = _dot(fb, w1_ref[0, half:, :])
    hb_next = jnp.concatenate([hb[1:], hb[:1]], axis=0)
    hid = _gelu(ha + hb_next)
    out_ref[0, 0, 0] = _dot(hid.astype(BF16), w2_ref[0]).astype(BF16)


def _compress(fc, pos_flat, w1, w2):
    bsz, _, _, n_chunk, width = fc.shape
    return pl.pallas_call(
        _compress_kernel,
        grid=(bsz, 2, NSA_KV),
        in_specs=[
            pl.BlockSpec((1, 1, 1, n_chunk, width), lambda b, k, g: (b, k, g, 0, 0)),
            pl.BlockSpec((1, 1, CMP_LEN * HEAD_DIM), lambda b, k, g: (k, 0, 0)),
            pl.BlockSpec((1, CMP_LEN * HEAD_DIM, CMP_HIDDEN), lambda b, k, g: (k, 0, 0)),
            pl.BlockSpec((1, CMP_HIDDEN, HEAD_DIM), lambda b, k, g: (k, 0, 0)),
        ],
        out_specs=pl.BlockSpec((1, 1, 1, n_chunk, HEAD_DIM), lambda b, k, g: (b, k, g, 0, 0)),
        out_shape=jax.ShapeDtypeStruct((bsz, 2, NSA_KV, n_chunk, HEAD_DIM), BF16),
        compiler_params=pltpu.CompilerParams(
            dimension_semantics=("arbitrary", "arbitrary", "arbitrary"), vmem_limit_bytes=VMEM_LIMIT),
        name="compress",
    )(fc, pos_flat, w1, w2)


def _nsa_kernel(q_ref, zc_ref, gl_ref, kcv_ref, ksa_ref, vs_ref, kw_ref, vw_ref, ovt_ref, bias_ref, y_ref, s_buf,
                m_ref, acc_ref, *, n_sel):
    qi = pl.program_id(2)
    tq = q_ref.shape[1]
    rows = NSA_GROUP * tq
    s0 = qi * tq
    q = q_ref[0]
    q_heads = [q[:, h * HEAD_DIM:(h + 1) * HEAD_DIM] for h in range(NSA_GROUP)]
    qs = jnp.concatenate(q_heads, axis=0)
    t_q = s0 + lax.broadcasted_iota(jnp.int32, (tq, 1), 0)

    blk = lax.broadcasted_iota(jnp.int32, (SEL_LANES, 1), 0)
    blk_f = blk.astype(F32)

    def compressed_branch():
        kc = kcv_ref[0, 0, 0]
        vc = kcv_ref[0, 1, 0]
        n_cmp = kc.shape[0]
        cmp_end = lax.broadcasted_iota(jnp.int32, (1, n_cmp), 1) * CMP_STRIDE + (CMP_LEN - 1)
        valid_c = cmp_end <= t_q
        any_valid = t_q >= CMP_LEN - 1
        o_c = []
        ps = None
        for h in range(NSA_GROUP):
            sc = jnp.where(valid_c, _dot_nt(q_heads[h], kc), NEG)
            m = jnp.max(sc, axis=-1, keepdims=True)
            p = jnp.exp2(sc - m)
            l = jnp.sum(p, axis=-1, keepdims=True)
            pn = p * jnp.where(any_valid, 1.0 / l, 0.0)
            o_c.append(_dot(pn.astype(BF16), vc))
            ps = pn if ps is None else ps + pn
        ps_hi = ps.astype(BF16)
        ps_lo = (ps - ps_hi.astype(F32)).astype(BF16)
        imp = _dot_nt(ovt_ref[...], ps_hi) + _dot_nt(ovt_ref[...], ps_lo)
        cur = (s0 + lax.broadcasted_iota(jnp.int32, (1, tq), 1)) // SLC_BLOCK
        future = blk > cur
        forced = (blk == 0) | (blk == cur) | (blk == cur - 1)
        return jnp.where(future, -1.0, imp + jnp.where(forced, FORCE_BONUS, 0.0)), tuple(o_c)

    def knock_out(work_i, rounds):
        for _ in range(rounds):
            mx = jnp.max(work_i, axis=0, keepdims=True)
            idx = jnp.min(jnp.where(work_i == mx, blk_f, float(SEL_LANES)), axis=0, keepdims=True)
            work_i = jnp.where(blk_f == idx, -jnp.inf, work_i)
        return work_i

    n_tiles = ksa_ref.shape[2] // TK

    def tile_start(kt):
        return pl.multiple_of(jnp.clip(kt, 0, n_tiles - 1) * TK, TK)

    one = 1 + jnp.minimum(qi, 0)

    def run_tiles(q_op, k_ref, v_ref, kt0, stride, n_plain, n_masked, bias_fn, side_fn=None, side=None):
        def tile(ref, i):
            return ref[0, 0, pl.ds(tile_start(kt0 + stride * i), TK), :]

        def step(i, slot, state, prefetch, masked, pos=0):
            m_i, acc, side_i = state
            s = s_buf[slot]
            if prefetch:
                s_buf[1 - slot] = _dot_nt(q_op, tile(k_ref, i + 1))
            if masked:
                s = (s.reshape(NSA_GROUP, tq, TK) + bias_fn(pos)[None]).reshape(rows, TK)
            m_new = jnp.maximum(m_i, jnp.broadcast_to(jnp.max(s, axis=-1, keepdims=True), m_i.shape))
            p_t = jnp.exp2(s - jnp.concatenate([m_new] * (TK // LANES), axis=1)).astype(BF16)
            acc = jnp.exp2(m_i - m_new) * acc + _dot(p_t, tile(v_ref, i))
            return m_new, acc, (side_fn(pos, side_i) if side_fn is not None else side_i)

        def group(width, first, masked, prefetch_last=True, slot0=0):
            def body(j, side_i):
                state = (m_ref[...], acc_ref[...], side_i)
                for i in range(width):
                    slot = slot0 if width == 1 else i % 2
                    state = step(first + width * j + i, slot, state, prefetch_last or i + 1 < width, masked, i)
                m_ref[...] = state[0]
                acc_ref[...] = state[1]
                return state[2]
            return body

        s_buf[0] = _dot_nt(q_op, tile(k_ref, 0))
        m_ref[...] = jnp.full(m_ref.shape, NEG, F32)
        acc_ref[...] = jnp.zeros(acc_ref.shape, F32)
        if isinstance(n_plain, int):
            assert n_plain == 0 and n_masked == 2
            side = lax.fori_loop(0, one, group(2, 0, True, prefetch_last=False), side)
        else:
            assert n_masked == 1
            n4 = n_plain // 4
            side = lax.fori_loop(0, n4, group(4, 0, False), side)
            side = lax.fori_loop(0, (n_plain % 4) // 2, group(2, 4 * n4, False), side)
            side = lax.fori_loop(0, n_plain % 2, group(1, n_plain - 1, False), side)
            side = lax.fori_loop(0, one, group(1, n_plain, True, prefetch_last=False, slot0=n_plain % 2), side)
        acc = acc_ref[...]
        return acc[:, :HEAD_DIM] * (1.0 / acc[:, HEAD_DIM:HEAD_DIM + 1]), side

    assert tq == TK == WINDOW
    older = jnp.where(qi > 0, bias_ref[1], NEG)

    assert n_sel % 2 == 0
    work, o_c = compressed_branch()
    o_w, work = run_tiles(qs, kw_ref, vw_ref, qi, -1, 0, 2, lambda i: bias_ref[0] if i == 0 else older,
                          side_fn=lambda pos, w: knock_out(w, n_sel // 2), side=work)
    sel_t = jnp.where(work == -jnp.inf, 0.0, -1.0)
    sel_rows = jnp.concatenate([sel_t.T.astype(BF16)] * NSA_GROUP, axis=0)
    q_aug = jnp.concatenate([sel_rows, qs], axis=1)

    o_s, _ = run_tiles(q_aug, ksa_ref, vs_ref, 0, 1, qi, 1, lambda i: bias_ref[0])

    gates = _sigmoid(gl_ref[0])
    heads = []
    for h in range(NSA_GROUP):
        y_h = o_c[h] * gates[:, 3 * h:3 * h + 1]
        y_h = y_h + o_s[h * tq:(h + 1) * tq] * gates[:, 3 * h + 1:3 * h + 2]
        y_h = y_h + o_w[h * tq:(h + 1) * tq] * gates[:, 3 * h + 2:3 * h + 3]
        heads.append(y_h)
    y = jnp.concatenate(heads, axis=1)
    y_ref[0] = (y * _silu(zc_ref[0])).astype(BF16)


def _nsa(q, zc, gl, kcv, ksa, vs, kw, vw, overlap, tile_bias):
    bsz, seq, _ = q.shape
    n_cmp = kcv.shape[3]
    gw = NSA_GROUP * HEAD_DIM
    n_sel = min(N_SELECT, seq // SLC_BLOCK)
    per_bg = lambda b, g, i: (b, g, 0, 0)
    return pl.pallas_call(
        functools.partial(_nsa_kernel, n_sel=n_sel),
        grid=(bsz, NSA_KV, seq // TQ),
        in_specs=[
            pl.BlockSpec((1, TQ, gw), lambda b, g, i: (b, i, g)),
            pl.BlockSpec((1, TQ, gw), lambda b, g, i: (b, i, g)),
            pl.BlockSpec((1, TQ, LANES), lambda b, g, i: (b, i, g)),
            pl.BlockSpec((1, 2, 1, n_cmp, HEAD_DIM), lambda b, g, i: (b, 0, g, 0, 0)),
            pl.BlockSpec((1, 1, seq, SEL_LANES + HEAD_DIM), per_bg, pipeline_mode=pl.Buffered(1)),
            pl.BlockSpec((1, 1, seq, 2 * HEAD_DIM), per_bg, pipeline_mode=pl.Buffered(1)),
            pl.BlockSpec((1, 1, seq, HEAD_DIM), per_bg, pipeline_mode=pl.Buffered(1)),
            pl.BlockSpec((1, 1, seq, 2 * HEAD_DIM), per_bg, pipeline_mode=pl.Buffered(1)),
            pl.BlockSpec(overlap.shape, lambda b, g, i: (0, 0), pipeline_mode=pl.Buffered(1)),
            pl.BlockSpec(tile_bias.shape, lambda b, g, i: (0, 0, 0), pipeline_mode=pl.Buffered(1)),
        ],
        out_specs=pl.BlockSpec((1, TQ, gw), lambda b, g, i: (b, i, g)),
        out_shape=jax.ShapeDtypeStruct((bsz, seq, NSA_WIDTH), BF16),
        scratch_shapes=[pltpu.VMEM((2, NSA_GROUP * TQ, TK), F32),
                        pltpu.VMEM((NSA_GROUP * TQ, LANES), F32),
                        pltpu.VMEM((NSA_GROUP * TQ, 2 * HEAD_DIM), F32)],
        compiler_params=pltpu.CompilerParams(
            dimension_semantics=("arbitrary", "arbitrary", "arbitrary"), vmem_limit_bytes=VMEM_LIMIT),
        name="nsa",
    )(q, zc, gl, kcv, ksa, vs, kw, vw, overlap, tile_bias)


def _final_kernel(x_ref, part_ref, yc_ref, wout_ref, g_ref, b_ref, o_ref, *, alpha):
    o_ref[0] = _layer_output(x_ref[0], part_ref[0], yc_ref[0], wout_ref[...], g_ref[...], b_ref[...], alpha)


def _final(x, part, yc, wout_c, ln_g, ln_b, alpha):
    bsz, seq, dm = x.shape
    ts = TS_PROJ
    tok = lambda b, s: (b, s, 0)
    return pl.pallas_call(
        functools.partial(_final_kernel, alpha=alpha),
        grid=(bsz, seq // ts),
        in_specs=[
            pl.BlockSpec((1, ts, dm), tok),
            pl.BlockSpec((1, ts, dm), tok),
            pl.BlockSpec((1, ts, NSA_WIDTH), tok),
            pl.BlockSpec(wout_c.shape, lambda b, s: (0, 0)),
            pl.BlockSpec(ln_g.shape, lambda b, s: (0, 0)),
            pl.BlockSpec(ln_b.shape, lambda b, s: (0, 0)),
        ],
        out_specs=pl.BlockSpec((1, ts, dm), tok),
        out_shape=jax.ShapeDtypeStruct((bsz, seq, dm), F32),
        compiler_params=pltpu.CompilerParams(
            dimension_semantics=("arbitrary", "arbitrary"), vmem_limit_bytes=VMEM_LIMIT),
        name="final",
    )(x, part, yc, wout_c, ln_g, ln_b)


def _pack_w_tail(w_in):
    per_group = N_GATES // NSA_KV
    gates = w_in[:, :, COL_TAIL:COL_TAIL + N_GATES]
    blocks = []
    for g in range(NSA_KV):
        blk = gates[:, :, g * per_group:(g + 1) * per_group]
        blocks.append(jnp.pad(blk, ((0, 0), (0, 0), (0, LANES - per_group))))
    blocks.append(w_in[:, :, COL_TAIL + N_GATES:])
    return jnp.concatenate(blocks, axis=-1).astype(BF16)


def _block_diag_gates(wa, wx):
    depth = wa.shape[0]
    per_half = LRU_BLOCKS // 2
    half = per_half * LRU_BLOCK
    halves = []
    for hf in range(2):
        mats = []
        for w in (wa, wx):
            m = jnp.zeros((depth, half, half), F32)
            for j in range(per_half):
                m = m.at[:, j * LRU_BLOCK:(j + 1) * LRU_BLOCK, j * LRU_BLOCK:(j + 1) * LRU_BLOCK].set(
                    w[:, hf * per_half + j])
            mats.append(m)
        halves.append(jnp.concatenate(mats, axis=-1))
    return jnp.stack(halves, axis=1).astype(BF16)


def _overlap_matrix(n_cmp_pad, n_cmp, n_slc):
    ci = np.arange(n_cmp_pad)[:, None] * CMP_STRIDE
    sj = np.arange(SEL_LANES)[None, :] * SLC_BLOCK
    ov = (ci < sj + SLC_BLOCK) & (ci + CMP_LEN > sj)
    ov &= (np.arange(n_cmp_pad)[:, None] < n_cmp) & (np.arange(SEL_LANES)[None, :] < n_slc)
    return jnp.asarray(ov.T.astype(np.float32), dtype=BF16)


def _tile_bias():
    r = np.arange(TQ)[:, None]
    k = np.arange(TK)[None, :]
    diag = np.where(k <= r, 0.0, NEG)
    older = np.where(k > r, 0.0, NEG)
    return jnp.asarray(np.stack([diag, older]), dtype=F32)


def kernel(x, mem, w_in, sgu_ln_g, sgu_ln_b, sgu_w, sgu_b, conv_w, conv_b, lru_wa, lru_ba, lru_wx, lru_bx,
           lru_lambda, cmp_pos, cmp_w1, cmp_w2, w_mem_kv, w_out, ln_g, ln_b):
    depth = w_in.shape[0]
    bsz, seq, dm = x.shape
    assert TQ == TK and WINDOW <= TK
    assert seq % TK == 0 and seq // SLC_BLOCK <= SEL_LANES
    alpha = (2 * depth) ** 0.25
    n_chunk = seq // CMP_STRIDE
    n_cmp = (seq - CMP_LEN) // CMP_STRIDE + 1

    w_in_b = w_in.astype(BF16)
    w_tail = _pack_w_tail(w_in)
    wbd = _block_diag_gates(lru_wa, lru_wx)
    wmem = w_mem_kv.astype(BF16)
    c0 = A_WIDTH + LRU_WIDTH
    wout_abd = jnp.concatenate([w_out[:, :c0], w_out[:, c0 + NSA_WIDTH:]], axis=1).astype(BF16)
    wout_c = w_out[:, c0:c0 + NSA_WIDTH].astype(BF16)
    w1 = cmp_w1.astype(BF16)
    w2 = cmp_w2.astype(BF16)
    pos_flat = cmp_pos.reshape(depth, 2, 1, CMP_LEN * HEAD_DIM)
    overlap = _overlap_matrix(n_chunk, n_cmp, seq // SLC_BLOCK)
    tile_bias = _tile_bias()
    row = lambda a: a[:, None, :]

    prev = None
    for l in range(depth):
        outs = _proj_mix(
            x, prev, alpha, mem, w_in_b[l], w_tail[l], row(sgu_ln_g)[l], row(sgu_ln_b)[l], sgu_w[l], sgu_b[l][:, :, None],
            conv_w[l], row(conv_b)[l], wbd[l], row(lru_ba)[l], row(lru_bx)[l], row(lru_lambda)[l],
            wmem[l], wout_abd[l])
        if prev is not None:
            x, outs = outs[0], outs[1:]
        part, q, zc, gl, kvc_raw, ksa, vs, kw, vw = outs
        fc = kvc_raw.reshape(bsz, 2, NSA_KV, n_chunk, CMP_STRIDE * HEAD_DIM)
        kcv = _compress(fc, pos_flat[l], w1[l], w2[l])
        yc = _nsa(q, zc, gl, kcv, ksa, vs, kw, vw, overlap, tile_bias)
        prev = (part, yc, wout_c[l], row(ln_g)[l], row(ln_b)[l])
    return _final(x, *prev, alpha)
```

```python
import functools

import jax
import jax.numpy as jnp
import numpy as np
from jax import lax
from jax.experimental import pallas as pl
from jax.experimental.pallas import tpu as pltpu

F32 = jnp.float32
BF16 = jnp.bfloat16

CHUNK = 128
A_GROUPS = 4
A_WIDTH = 512
LRU_WIDTH = 512
LRU_BLOCKS = 8
LRU_BLOCK = 64
CONV_W = 4
C_LRU = 8.0
HEAD_DIM = 64
NSA_KV = 2
NSA_GROUP = 4
NSA_WIDTH = 512
KV_WIDTH = 128
CMP_LEN = 32
CMP_STRIDE = 16
CMP_HIDDEN = 128
SLC_BLOCK = 64
N_SELECT = 16
WINDOW = 512
N_GATES = 24
MEM_HEADS = 4
MEM_WIDTH = 256
LN_EPS = 1e-5
NEG = -1e30
FORCE_BONUS = 1e4

LANES = 128
SUBLANES = 8
SEL_LANES = 128
VMEM_LIMIT = 56 * 1024 * 1024

COL_A = 0
COL_B = 1536
COL_C = 2560
COL_TAIL = 4352
TAIL_G = 0
TAIL_D = 256

TS_PROJ = 512
TQ = 512
TK = 512

Q_SCALE = HEAD_DIM ** -0.5 * float(np.log2(np.e))


def _sigmoid(x):
    return 1.0 / (1.0 + jnp.exp(-x))


def _silu(x):
    return x * _sigmoid(x)


def _gelu(x):
    return jax.nn.gelu(x, approximate=True)


def _dot(a, b):
    return jnp.dot(a, b, preferred_element_type=F32)


def _dot_nt(a, b):
    return lax.dot_general(a, b, (((1,), (1,)), ((), ())), preferred_element_type=F32)


def _layer_output(x, part, yc, wout_c, g, b, alpha):
    y = alpha * x + (part + _dot(yc, wout_c))
    mu = jnp.mean(y, axis=-1, keepdims=True)
    var = jnp.mean(jnp.square(y - mu), axis=-1, keepdims=True)
    return ((y - mu) * lax.rsqrt(var + LN_EPS)) * g + b


def _proj_mix_kernel(*refs, alpha):
    if alpha is not None:
        x_ref, part_p_ref, yc_p_ref, woutc_p_ref, lng_p_ref, lnb_p_ref = refs[:6]
        refs = refs[6:]
    else:
        x_ref, refs = refs[0], refs[1:]
    (mem_ref, w_in_ref, w_tail_ref, sln_g_ref, sln_b_ref, sw_ref, sb_ref, cw_ref, cb_ref, wbd_ref, ba_ref, bx_ref,
     lam_ref, wmem_ref, wout_ref) = refs[:15]
    refs = refs[15:]
    if alpha is not None:
        xo_ref, refs = refs[0], refs[1:]
    (part_ref, q_ref, zc_ref, gl_ref, kvc_ref, ksa_ref, vs_ref, kw_ref, vw_ref,
     conv_buf, h_carry, memk, memv) = refs
    s = pl.program_id(1)
    ts = x_ref.shape[1]
    if alpha is not None:
        x_new = _layer_output(x_ref[0], part_p_ref[0], yc_p_ref[0], woutc_p_ref[...], lng_p_ref[...],
                              lnb_p_ref[...], alpha)
        xo_ref[0] = x_new
        xb = x_new.astype(BF16)
    else:
        xb = x_ref[0].astype(BF16)

    @pl.when(s == 0)
    def _():
        conv_buf[0:8, :] = jnp.zeros((8, LRU_WIDTH), F32)
        h_carry[...] = jnp.zeros_like(h_carry)
        kv = _dot(mem_ref[0].astype(BF16), wmem_ref[...])
        memk[...] = kv[:, :MEM_WIDTH].astype(BF16)
        memv[...] = kv[:, MEM_WIDTH:].astype(BF16)

    pa = _dot(xb, w_in_ref[:, COL_A:COL_A + 3 * A_WIDTH])
    u = _gelu(pa[:, 0:A_WIDTH])
    v = _gelu(pa[:, A_WIDTH:2 * A_WIDTH])
    za = pa[:, 2 * A_WIDTH:3 * A_WIDTH]
    row = lax.broadcasted_iota(jnp.int32, (CHUNK, CHUNK), 0)
    col = lax.broadcasted_iota(jnp.int32, (CHUNK, CHUNK), 1)
    causal = col <= row
    sv_groups = []
    for g in range(A_GROUPS):
        vg = v[:, g * CHUNK:(g + 1) * CHUNK]
        mu = jnp.mean(vg, axis=-1, keepdims=True)
        var = jnp.mean(jnp.square(vg - mu), axis=-1, keepdims=True)
        vn = (vg - mu) * lax.rsqrt(var + LN_EPS)
        vn = vn * sln_g_ref[:, g * CHUNK:(g + 1) * CHUNK] + sln_b_ref[:, g * CHUNK:(g + 1) * CHUNK]
        vnb = vn.astype(BF16)
        wg = jnp.where(causal, sw_ref[g], 0.0).astype(BF16)
        bias = sb_ref[g]
        chunks = []
        for c in range(ts // CHUNK):
            chunks.append(_dot(wg, vnb[c * CHUNK:(c + 1) * CHUNK]) + bias)
        sv_groups.append(jnp.concatenate(chunks, axis=0))
    sv = jnp.concatenate(sv_groups, axis=1)
    ya = (u * sv) * _silu(za)

    pb = _dot(xb, w_in_ref[:, COL_B:COL_B + 2 * LRU_WIDTH])
    xbv = pb[:, :LRU_WIDTH]
    zb = pb[:, LRU_WIDTH:]
    conv_buf[8:8 + ts, :] = xbv
    xc = cb_ref[...] + conv_buf[5:5 + ts, :] * cw_ref[0:1, :]
    xc = xc + conv_buf[6:6 + ts, :] * cw_ref[1:2, :]
    xc = xc + conv_buf[7:7 + ts, :] * cw_ref[2:3, :]
    xc = xc + xbv * cw_ref[3:4, :]
    conv_buf[0:8, :] = xbv[ts - 8:ts, :]
    xcb = xc.astype(BF16)
    half = LRU_WIDTH // 2
    g0 = _dot(xcb[:, :half], wbd_ref[0])
    g1 = _dot(xcb[:, half:], wbd_ref[1])
    r = _sigmoid(jnp.concatenate([g0[:, :half], g1[:, :half]], axis=1) + ba_ref[...])
    ig = _sigmoid(jnp.concatenate([g0[:, half:], g1[:, half:]], axis=1) + bx_ref[...])
    nlam = -lam_ref[...]
    softplus = jnp.maximum(nlam, 0.0) + jnp.log1p(jnp.exp(-jnp.abs(nlam)))
    log_a = (-C_LRU * r) * softplus
    a_cum = jnp.exp(log_a)
    th = jnp.tanh(log_a)
    b_cum = jnp.sqrt((-2.0 * th) / (1.0 - th)) * (ig * xc)
    sub = SUBLANES
    a_cum = a_cum.reshape(ts // sub, sub, LRU_WIDTH)
    b_cum = b_cum.reshape(ts // sub, sub, LRU_WIDTH)
    row_in_block = lax.broadcasted_iota(jnp.int32, (1, sub, 1), 1)
    d = 1
    while d < sub:
        keep = row_in_block >= d
        a_sh = jnp.where(keep, pltpu.roll(a_cum, d, axis=1), 1.0)
        b_sh = jnp.where(keep, pltpu.roll(b_cum, d, axis=1), 0.0)
        b_cum = b_cum + a_cum * b_sh
        a_cum = a_cum * a_sh
        d *= 2
    carry = h_carry[...]
    blocks = []
    for blk in range(ts // sub):
        hb = b_cum[blk] + a_cum[blk] * carry
        carry = hb[sub - 1:sub, :]
        blocks.append(hb)
    h = jnp.concatenate(blocks, axis=0)
    h_carry[...] = carry
    yb = h * _silu(zb)

    pd = _dot(xb, w_tail_ref[:, TAIL_D:TAIL_D + 2 * MEM_WIDTH])
    qd = pd[:, :MEM_WIDTH] * (HEAD_DIM ** -0.5)
    zd = pd[:, MEM_WIDTH:]
    head_of_lane = lax.broadcasted_iota(jnp.int32, (1, MEM_WIDTH), 1) // HEAD_DIM
    od = jnp.zeros((ts, MEM_WIDTH), F32)
    for hd in range(MEM_HEADS):
        hm = head_of_lane == hd
        qh = jnp.where(hm, qd, 0.0).astype(BF16)
        sc = _dot_nt(qh, memk[...])
        m = jnp.max(sc, axis=-1, keepdims=True)
        p = jnp.exp(sc - m)
        l = jnp.sum(p, axis=-1, keepdims=True)
        pv = _dot(p.astype(BF16), memv[...])
        od = od + jnp.where(hm, pv * (1.0 / l), 0.0)
    yd = od * _silu(zd)

    y_abd = jnp.concatenate([ya, yb, yd], axis=1).astype(BF16)
    part_ref[0] = _dot(y_abd, wout_ref[...])

    pc = _dot(xb, w_in_ref[:, COL_C:COL_C + 2 * NSA_WIDTH + 6 * KV_WIDTH])
    q_ref[0] = (pc[:, :NSA_WIDTH] * Q_SCALE).astype(BF16)
    zc_ref[0] = pc[:, NSA_WIDTH:2 * NSA_WIDTH]
    gl_ref[0] = _dot(xb, w_tail_ref[:, TAIL_G:TAIL_G + NSA_KV * LANES])
    base = 2 * NSA_WIDTH
    kpos = s * ts + lax.broadcasted_iota(jnp.int32, (ts, SEL_LANES), 0)
    blk = lax.broadcasted_iota(jnp.int32, (ts, SEL_LANES), 1)
    sel_rows = jnp.where((kpos // SLC_BLOCK) == blk, -NEG, 0.0).astype(BF16)
    ones_col = jnp.where(lax.broadcasted_iota(jnp.int32, (ts, HEAD_DIM), 1) == 0, 1.0, 0.0).astype(BF16)
    for g in range(NSA_KV):
        lo = g * HEAD_DIM
        kvc_ref[0, 0, g] = pc[:, base + lo:base + lo + HEAD_DIM].astype(BF16)
        kvc_ref[0, 1, g] = pc[:, base + KV_WIDTH + lo:base + KV_WIDTH + lo + HEAD_DIM].astype(BF16)
        ksa_ref[0, g, :, 0:SEL_LANES] = sel_rows
        ksa_ref[0, g, :, SEL_LANES:SEL_LANES + HEAD_DIM] = (
            pc[:, base + 2 * KV_WIDTH + lo:base + 2 * KV_WIDTH + lo + HEAD_DIM].astype(BF16))
        vs_ref[0, g, :, 0:HEAD_DIM] = (
            pc[:, base + 3 * KV_WIDTH + lo:base + 3 * KV_WIDTH + lo + HEAD_DIM].astype(BF16))
        vs_ref[0, g, :, HEAD_DIM:2 * HEAD_DIM] = ones_col
        kw_ref[0, g] = pc[:, base + 4 * KV_WIDTH + lo:base + 4 * KV_WIDTH + lo + HEAD_DIM].astype(BF16)
        vw_ref[0, g, :, 0:HEAD_DIM] = (
            pc[:, base + 5 * KV_WIDTH + lo:base + 5 * KV_WIDTH + lo + HEAD_DIM].astype(BF16))
        vw_ref[0, g, :, HEAD_DIM:2 * HEAD_DIM] = ones_col


def _proj_mix(x, prev, alpha, layer, mem, w_in_b, w_tail, sln_g, sln_b, sw, sb, cw, cb, wbd, ba, bx, lam, wmem,
              wout_abd):
    bsz, seq, dm = x.shape
    ts = TS_PROJ
    n_mem = mem.shape[1]
    tok = lambda b, s: (b, s, 0)
    stacked = (w_in_b, w_tail, wbd, wmem, wout_abd)

    def whole(a):
        if any(a is st for st in stacked):
            return pl.BlockSpec((None,) + a.shape[1:], lambda b, s: (layer,) + (0,) * (a.ndim - 1),
                                pipeline_mode=pl.Buffered(1))
        return pl.BlockSpec(a.shape, lambda b, s: (0,) * a.ndim, pipeline_mode=pl.Buffered(1))

    consts = (w_in_b, w_tail, sln_g, sln_b, sw, sb, cw, cb, wbd, ba, bx, lam, wmem, wout_abd)
    args = [x]
    in_specs = [pl.BlockSpec((1, ts, dm), tok)]
    if prev is not None:
        part_p, yc_p, wout_c_p, ln_g_p, ln_b_p = prev
        args += [part_p, yc_p, wout_c_p, ln_g_p, ln_b_p]
        in_specs += [pl.BlockSpec((1, ts, dm), tok), pl.BlockSpec((1, ts, NSA_WIDTH), tok),
                     whole(wout_c_p), whole(ln_g_p), whole(ln_b_p)]
    args += [mem, *consts]
    in_specs += [pl.BlockSpec((1, n_mem, dm), lambda b, s: (b, 0, 0), pipeline_mode=pl.Buffered(1))]
    in_specs += [whole(a) for a in consts]
    kv_spec = pl.BlockSpec((1, NSA_KV, ts, HEAD_DIM), lambda b, s: (b, 0, s, 0))
    va_spec = pl.BlockSpec((1, NSA_KV, ts, 2 * HEAD_DIM), lambda b, s: (b, 0, s, 0))
    out_specs = [
        pl.BlockSpec((1, ts, dm), lambda b, s: (b, s, 0)),
        pl.BlockSpec((1, ts, NSA_WIDTH), lambda b, s: (b, s, 0)),
        pl.BlockSpec((1, ts, NSA_WIDTH), lambda b, s: (b, s, 0)),
        pl.BlockSpec((1, ts, NSA_KV * LANES), lambda b, s: (b, s, 0)),
        pl.BlockSpec((1, 2, NSA_KV, ts, HEAD_DIM), lambda b, s: (b, 0, 0, s, 0)),
        pl.BlockSpec((1, NSA_KV, ts, SEL_LANES + HEAD_DIM), lambda b, s: (b, 0, s, 0)),
        va_spec, kv_spec, va_spec,
    ]
    out_shape = [
        jax.ShapeDtypeStruct((bsz, seq, dm), F32),
        jax.ShapeDtypeStruct((bsz, seq, NSA_WIDTH), BF16),
        jax.ShapeDtypeStruct((bsz, seq, NSA_WIDTH), F32),
        jax.ShapeDtypeStruct((bsz, seq, NSA_KV * LANES), F32),
        jax.ShapeDtypeStruct((bsz, 2, NSA_KV, seq, HEAD_DIM), BF16),
        jax.ShapeDtypeStruct((bsz, NSA_KV, seq, SEL_LANES + HEAD_DIM), BF16),
        jax.ShapeDtypeStruct((bsz, NSA_KV, seq, 2 * HEAD_DIM), BF16),
        jax.ShapeDtypeStruct((bsz, NSA_KV, seq, HEAD_DIM), BF16),
        jax.ShapeDtypeStruct((bsz, NSA_KV, seq, 2 * HEAD_DIM), BF16),
    ]
    if prev is not None:
        out_specs = [pl.BlockSpec((1, ts, dm), tok)] + out_specs
        out_shape = [jax.ShapeDtypeStruct((bsz, seq, dm), F32)] + out_shape
    return pl.pallas_call(
        functools.partial(_proj_mix_kernel, alpha=alpha if prev is not None else None),
        grid=(bsz, seq // ts),
        in_specs=in_specs,
        out_specs=out_specs,
        out_shape=out_shape,
        scratch_shapes=[
            pltpu.VMEM((ts + 8, LRU_WIDTH), F32),
            pltpu.VMEM((1, LRU_WIDTH), F32),
            pltpu.VMEM((n_mem, MEM_WIDTH), BF16),
            pltpu.VMEM((n_mem, MEM_WIDTH), BF16),
        ],
        compiler_params=pltpu.CompilerParams(
            dimension_semantics=("arbitrary", "arbitrary"), vmem_limit_bytes=VMEM_LIMIT),
        name="proj_mix",
    )(*args)


def _compress_kernel(fc_ref, pos_ref, w1_ref, w2_ref, out_ref):
    fc = fc_ref[0, 0, 0].astype(F32)
    half = (CMP_LEN // 2) * HEAD_DIM
    pos = pos_ref[0]
    fa = (fc + pos[:, :half]).astype(BF16)
    fb = (fc + pos[:, half:]).astype(BF16)
    ha = _dot(fa, w1_ref[0, :half, :])
    hb = _dot(fb, w1_ref[0, half:, :])
    hb_next = jnp.concatenate([hb[1:], hb[:1]], axis=0)
    hid = _gelu(ha + hb_next)
    out_ref[0, 0, 0] = _dot(hid.astype(BF16), w2_ref[0]).astype(BF16)


def _compress(fc, pos_flat, w1, w2):
    bsz, _, _, n_chunk, width = fc.shape
    return pl.pallas_call(
        _compress_kernel,
        grid=(bsz, 2, NSA_KV),
        in_specs=[
            pl.BlockSpec((1, 1, 1, n_chunk, width), lambda b, k, g: (b, k, g, 0, 0)),
            pl.BlockSpec((1, 1, CMP_LEN * HEAD_DIM), lambda b, k, g: (k, 0, 0)),
            pl.BlockSpec((1, CMP_LEN * HEAD_DIM, CMP_HIDDEN), lambda b, k, g: (k, 0, 0)),
            pl.BlockSpec((1, CMP_HIDDEN, HEAD_DIM), lambda b, k, g: (k, 0, 0)),
        ],
        out_specs=pl.BlockSpec((1, 1, 1, n_chunk, HEAD_DIM), lambda b, k, g: (b, k, g, 0, 0)),
        out_shape=jax.ShapeDtypeStruct((bsz, 2, NSA_KV, n_chunk, HEAD_DIM), BF16),
        compiler_params=pltpu.CompilerParams(
            dimension_semantics=("arbitrary", "arbitrary", "arbitrary"), vmem_limit_bytes=VMEM_LIMIT),
        name="compress",
    )(fc, pos_flat, w1, w2)


def _nsa_kernel(q_ref, zc_ref, gl_ref, kcv_ref, ksa_ref, vs_ref, kw_ref, vw_ref, ovt_ref, bias_ref, y_ref, s_buf,
                m_ref, acc_ref, *, n_sel):
    qi = pl.program_id(2)
    tq = q_ref.shape[1]
    rows = NSA_GROUP * tq
    s0 = qi * tq
    q = q_ref[0]
    q_heads = [q[:, h * HEAD_DIM:(h + 1) * HEAD_DIM] for h in range(NSA_GROUP)]
    qs = jnp.concatenate(q_heads, axis=0)
    t_q = s0 + lax.broadcasted_iota(jnp.int32, (tq, 1), 0)

    blk = lax.broadcasted_iota(jnp.int32, (SEL_LANES, 1), 0)
    blk_f = blk.astype(F32)

    def compressed_branch():
        kc = kcv_ref[0, 0, 0]
        vc = kcv_ref[0, 1, 0]
        n_cmp = kc.shape[0]
        cmp_end = lax.broadcasted_iota(jnp.int32, (1, n_cmp), 1) * CMP_STRIDE + (CMP_LEN - 1)
        valid_c = cmp_end <= t_q
        any_valid = t_q >= CMP_LEN - 1
        o_c = []
        ps = None
        for h in range(NSA_GROUP):
            sc = jnp.where(valid_c, _dot_nt(q_heads[h], kc), NEG)
            m = jnp.max(sc, axis=-1, keepdims=True)
            p = jnp.exp2(sc - m)
            l = jnp.sum(p, axis=-1, keepdims=True)
            pn = p * jnp.where(any_valid, 1.0 / l, 0.0)
            o_c.append(_dot(pn.astype(BF16), vc))
            ps = pn if ps is None else ps + pn
        ps_hi = ps.astype(BF16)
        ps_lo = (ps - ps_hi.astype(F32)).astype(BF16)
        imp = _dot_nt(ovt_ref[...], ps_hi) + _dot_nt(ovt_ref[...], ps_lo)
        cur = (s0 + lax.broadcasted_iota(jnp.int32, (1, tq), 1)) // SLC_BLOCK
        future = blk > cur
        forced = (blk == 0) | (blk == cur) | (blk == cur - 1)
        return jnp.where(future, -1.0, imp + jnp.where(forced, FORCE_BONUS, 0.0)), tuple(o_c)

    def knock_out(work_i, rounds):
        for _ in range(rounds):
            mx = jnp.max(work_i, axis=0, keepdims=True)
            idx = jnp.min(jnp.where(work_i == mx, blk_f, float(SEL_LANES)), axis=0, keepdims=True)
            work_i = jnp.where(blk_f == idx, -jnp.inf, work_i)
        return work_i

    n_tiles = ksa_ref.shape[2] // TK

    def tile_start(kt):
        return pl.multiple_of(jnp.clip(kt, 0, n_tiles - 1) * TK, TK)

    one = 1 + jnp.minimum(qi, 0)

    def run_tiles(q_op, k_ref, v_ref, kt0, stride, n_plain, n_masked, bias_fn, side_fn=None, side=None):
        def tile(ref, i):
            return ref[0, 0, pl.ds(tile_start(kt0 + stride * i), TK), :]

        def step(i, slot, state, prefetch, masked, pos=0):
            m_i, acc, side_i = state
            s = s_buf[slot]
            if prefetch:
                s_buf[1 - slot] = _dot_nt(q_op, tile(k_ref, i + 1))
            if masked:
                s = (s.reshape(NSA_GROUP, tq, TK) + bias_fn(pos)[None]).reshape(rows, TK)
            m_new = jnp.maximum(m_i, jnp.broadcast_to(jnp.max(s, axis=-1, keepdims=True), m_i.shape))
            p_t = jnp.exp2(s - jnp.concatenate([m_new] * (TK // LANES), axis=1)).astype(BF16)
            acc = jnp.exp2(m_i - m_new) * acc + _dot(p_t, tile(v_ref, i))
            return m_new, acc, (side_fn(pos, side_i) if side_fn is not None else side_i)

        def group(width, first, masked, prefetch_last=True, slot0=0):
            def body(j, side_i):
                state = (m_ref[...], acc_ref[...], side_i)
                for i in range(width):
                    slot = slot0 if width == 1 else i % 2
                    state = step(first + width * j + i, slot, state, prefetch_last or i + 1 < width, masked, i)
                m_ref[...] = state[0]
                acc_ref[...] = state[1]
                return state[2]
            return body

        s_buf[0] = _dot_nt(q_op, tile(k_ref, 0))
        m_ref[...] = jnp.full(m_ref.shape, NEG, F32)
        acc_ref[...] = jnp.zeros(acc_ref.shape, F32)
        if isinstance(n_plain, int):
            assert n_plain == 0 and n_masked == 2
            side = lax.fori_loop(0, one, group(2, 0, True, prefetch_last=False), side)
        else:
            assert n_masked == 1
            n4 = n_plain // 4
            side = lax.fori_loop(0, n4, group(4, 0, False), side)
            side = lax.fori_loop(0, (n_plain % 4) // 2, group(2, 4 * n4, False), side)
            side = lax.fori_loop(0, n_plain % 2, group(1, n_plain - 1, False), side)
            side = lax.fori_loop(0, one, group(1, n_plain, True, prefetch_last=False, slot0=n_plain % 2), side)
        acc = acc_ref[...]
        return acc[:, :HEAD_DIM] * (1.0 / acc[:, HEAD_DIM:HEAD_DIM + 1]), side

    assert tq == TK == WINDOW
    older = jnp.where(qi > 0, bias_ref[1], NEG)

    assert n_sel % 2 == 0
    work, o_c = compressed_branch()
    o_w, work = run_tiles(qs, kw_ref, vw_ref, qi, -1, 0, 2, lambda i: bias_ref[0] if i == 0 else older,
                          side_fn=lambda pos, w: knock_out(w, n_sel // 2), side=work)
    sel_t = jnp.where(work == -jnp.inf, 0.0, -1.0)
    sel_rows = jnp.concatenate([sel_t.T.astype(BF16)] * NSA_GROUP, axis=0)
    q_aug = jnp.concatenate([sel_rows, qs], axis=1)

    o_s, _ = run_tiles(q_aug, ksa_ref, vs_ref, 0, 1, qi, 1, lambda i: bias_ref[0])

    gates = _sigmoid(gl_ref[0])
    heads = []
    for h in range(NSA_GROUP):
        y_h = o_c[h] * gates[:, 3 * h:3 * h + 1]
        y_h = y_h + o_s[h * tq:(h + 1) * tq] * gates[:, 3 * h + 1:3 * h + 2]
        y_h = y_h + o_w[h * tq:(h + 1) * tq] * gates[:, 3 * h + 2:3 * h + 3]
        heads.append(y_h)
    y = jnp.concatenate(heads, axis=1)
    y_ref[0] = (y * _silu(zc_ref[0])).astype(BF16)


def _nsa(q, zc, gl, kcv, ksa, vs, kw, vw, overlap, tile_bias):
    bsz, seq, _ = q.shape
    n_cmp = kcv.shape[3]
    gw = NSA_GROUP * HEAD_DIM
    n_sel = min(N_SELECT, seq // SLC_BLOCK)
    per_bg = lambda b, g, i: (b, g, 0, 0)
    return pl.pallas_call(
        functools.partial(_nsa_kernel, n_sel=n_sel),
        grid=(bsz, NSA_KV, seq // TQ),
        in_specs=[
            pl.BlockSpec((1, TQ, gw), lambda b, g, i: (b, i, g)),
            pl.BlockSpec((1, TQ, gw), lambda b, g, i: (b, i, g)),
            pl.BlockSpec((1, TQ, LANES), lambda b, g, i: (b, i, g)),
            pl.BlockSpec((1, 2, 1, n_cmp, HEAD_DIM), lambda b, g, i: (b, 0, g, 0, 0)),
            pl.BlockSpec((1, 1, seq, SEL_LANES + HEAD_DIM), per_bg, pipeline_mode=pl.Buffered(1)),
            pl.BlockSpec((1, 1, seq, 2 * HEAD_DIM), per_bg, pipeline_mode=pl.Buffered(1)),
            pl.BlockSpec((1, 1, seq, HEAD_DIM), per_bg, pipeline_mode=pl.Buffered(1)),
            pl.BlockSpec((1, 1, seq, 2 * HEAD_DIM), per_bg, pipeline_mode=pl.Buffered(1)),
            pl.BlockSpec(overlap.shape, lambda b, g, i: (0, 0), pipeline_mode=pl.Buffered(1)),
            pl.BlockSpec(tile_bias.shape, lambda b, g, i: (0, 0, 0), pipeline_mode=pl.Buffered(1)),
        ],
        out_specs=pl.BlockSpec((1, TQ, gw), lambda b, g, i: (b, i, g)),
        out_shape=jax.ShapeDtypeStruct((bsz, seq, NSA_WIDTH), BF16),
        scratch_shapes=[pltpu.VMEM((2, NSA_GROUP * TQ, TK), F32),
                        pltpu.VMEM((NSA_GROUP * TQ, LANES), F32),
                        pltpu.VMEM((NSA_GROUP * TQ, 2 * HEAD_DIM), F32)],
        compiler_params=pltpu.CompilerParams(
            dimension_semantics=("arbitrary", "arbitrary", "arbitrary"), vmem_limit_bytes=VMEM_LIMIT),
        name="nsa",
    )(q, zc, gl, kcv, ksa, vs, kw, vw, overlap, tile_bias)


def _final_kernel(x_ref, part_ref, yc_ref, wout_ref, g_ref, b_ref, o_ref, *, alpha):
    o_ref[0] = _layer_output(x_ref[0], part_ref[0], yc_ref[0], wout_ref[...], g_ref[...], b_ref[...], alpha)


def _final(x, part, yc, wout_c, ln_g, ln_b, alpha):
    bsz, seq, dm = x.shape
    ts = TS_PROJ
    tok = lambda b, s: (b, s, 0)
    return pl.pallas_call(
        functools.partial(_final_kernel, alpha=alpha),
        grid=(bsz, seq // ts),
        in_specs=[
            pl.BlockSpec((1, ts, dm), tok),
            pl.BlockSpec((1, ts, dm), tok),
            pl.BlockSpec((1, ts, NSA_WIDTH), tok),
            pl.BlockSpec(wout_c.shape, lambda b, s: (0, 0)),
            pl.BlockSpec(ln_g.shape, lambda b, s: (0, 0)),
            pl.BlockSpec(ln_b.shape, lambda b, s: (0, 0)),
        ],
        out_specs=pl.BlockSpec((1, ts, dm), tok),
        out_shape=jax.ShapeDtypeStruct((bsz, seq, dm), F32),
        compiler_params=pltpu.CompilerParams(
            dimension_semantics=("arbitrary", "arbitrary"), vmem_limit_bytes=VMEM_LIMIT),
        name="final",
    )(x, part, yc, wout_c, ln_g, ln_b)


def _pack_w_tail(w_in):
    per_group = N_GATES // NSA_KV
    gates = w_in[:, :, COL_TAIL:COL_TAIL + N_GATES]
    blocks = []
    for g in range(NSA_KV):
        blk = gates[:, :, g * per_group:(g + 1) * per_group]
        blocks.append(jnp.pad(blk, ((0, 0), (0, 0), (0, LANES - per_group))))
    blocks.append(w_in[:, :, COL_TAIL + N_GATES:])
    return jnp.concatenate(blocks, axis=-1).astype(BF16)


def _block_diag_gates(wa, wx):
    depth = wa.shape[0]
    per_half = LRU_BLOCKS // 2
    half = per_half * LRU_BLOCK
    halves = []
    for hf in range(2):
        mats = []
        for w in (wa, wx):
            m = jnp.zeros((depth, half, half), F32)
            for j in range(per_half):
                m = m.at[:, j * LRU_BLOCK:(j + 1) * LRU_BLOCK, j * LRU_BLOCK:(j + 1) * LRU_BLOCK].set(
                    w[:, hf * per_half + j])
            mats.append(m)
        halves.append(jnp.concatenate(mats, axis=-1))
    return jnp.stack(halves, axis=1).astype(BF16)


def _overlap_matrix(n_cmp_pad, n_cmp, n_slc):
    ci = np.arange(n_cmp_pad)[:, None] * CMP_STRIDE
    sj = np.arange(SEL_LANES)[None, :] * SLC_BLOCK
    ov = (ci < sj + SLC_BLOCK) & (ci + CMP_LEN > sj)
    ov &= (np.arange(n_cmp_pad)[:, None] < n_cmp) & (np.arange(SEL_LANES)[None, :] < n_slc)
    return jnp.asarray(ov.T.astype(np.float32), dtype=BF16)


def _tile_bias():
    r = np.arange(TQ)[:, None]
    k = np.arange(TK)[None, :]
    diag = np.where(k <= r, 0.0, NEG)
    older = np.where(k > r, 0.0, NEG)
    return jnp.asarray(np.stack([diag, older]), dtype=F32)


def kernel(x, mem, w_in, sgu_ln_g, sgu_ln_b, sgu_w, sgu_b, conv_w, conv_b, lru_wa, lru_ba, lru_wx, lru_bx,
           lru_lambda, cmp_pos, cmp_w1, cmp_w2, w_mem_kv, w_out, ln_g, ln_b):
    depth = w_in.shape[0]
    bsz, seq, dm = x.shape
    assert TQ == TK and WINDOW <= TK
    assert seq % TK == 0 and seq // SLC_BLOCK <= SEL_LANES
    alpha = (2 * depth) ** 0.25
    n_chunk = seq // CMP_STRIDE
    n_cmp = (seq - CMP_LEN) // CMP_STRIDE + 1

    w_in_b = w_in.astype(BF16)
    w_tail = _pack_w_tail(w_in)
    wbd = _block_diag_gates(lru_wa, lru_wx)
    wmem = w_mem_kv.astype(BF16)
    c0 = A_WIDTH + LRU_WIDTH
    wout_abd = jnp.concatenate([w_out[:, :c0], w_out[:, c0 + NSA_WIDTH:]], axis=1).astype(BF16)
    wout_c = w_out[:, c0:c0 + NSA_WIDTH].astype(BF16)
    w1 = cmp_w1.astype(BF16)
    w2 = cmp_w2.astype(BF16)
    pos_flat = cmp_pos.reshape(depth, 2, 1, CMP_LEN * HEAD_DIM)
    overlap = _overlap_matrix(n_chunk, n_cmp, seq // SLC_BLOCK)
    tile_bias = _tile_bias()
    row = lambda a: a[:, None, :]

    prev = None
    for l in range(depth):
        outs = _proj_mix(
            x, prev, alpha, l, mem, w_in_b, w_tail, row(sgu_ln_g)[l], row(sgu_ln_b)[l], sgu_w[l],
            sgu_b[l][:, :, None], conv_w[l], row(conv_b)[l], wbd, row(lru_ba)[l], row(lru_bx)[l],
            row(lru_lambda)[l], wmem, wout_abd)
        if prev is not None:
            x, outs = outs[0], outs[1:]
        part, q, zc, gl, kvc_raw, ksa, vs, kw, vw = outs
        fc = kvc_raw.reshape(bsz, 2, NSA_KV, n_chunk, CMP_STRIDE * HEAD_DIM)
        kcv = _compress(fc, pos_flat[l], w1[l], w2[l])
        yc = _nsa(q, zc, gl, kcv, ksa, vs, kw, vw, overlap, tile_bias)
        prev = (part, yc, wout_c[l], row(ln_g)[l], row(ln_b)[l])
    return _final(x, *prev, alpha)
```

```python
import functools

import jax
import jax.numpy as jnp
import numpy as np
from jax import lax
from jax.experimental import pallas as pl
from jax.experimental.pallas import tpu as pltpu

F32 = jnp.float32
BF16 = jnp.bfloat16

CHUNK = 128
A_GROUPS = 4
A_WIDTH = 512
LRU_WIDTH = 512
LRU_BLOCKS = 8
LRU_BLOCK = 64
CONV_W = 4
C_LRU = 8.0
HEAD_DIM = 64
NSA_KV = 2
NSA_GROUP = 4
NSA_WIDTH = 512
KV_WIDTH = 128
CMP_LEN = 32
CMP_STRIDE = 16
CMP_HIDDEN = 128
SLC_BLOCK = 64
N_SELECT = 16
WINDOW = 512
N_GATES = 24
MEM_HEADS = 4
MEM_WIDTH = 256
LN_EPS = 1e-5
NEG = -1e30
FORCE_BONUS = 1e4

LANES = 128
SUBLANES = 8
SEL_LANES = 128
VMEM_LIMIT = 56 * 1024 * 1024

COL_A = 0
COL_B = 1536
COL_C = 2560
COL_TAIL = 4352
TAIL_G = 0
TAIL_D = 256

TS_PROJ = 512
TQ = 512
TK = 512

Q_SCALE = HEAD_DIM ** -0.5 * float(np.log2(np.e))


def _sigmoid(x):
    return 1.0 / (1.0 + jnp.exp(-x))


def _silu(x):
    return x * _sigmoid(x)


def _gelu(x):
    return jax.nn.gelu(x, approximate=True)


def _dot(a, b):
    return jnp.dot(a, b, preferred_element_type=F32)


def _dot_nt(a, b):
    return lax.dot_general(a, b, (((1,), (1,)), ((), ())), preferred_element_type=F32)


def _layer_output(x, part, yc, wout_c, g, b, alpha):
    y = alpha * x + (part + _dot(yc, wout_c))
    mu = jnp.mean(y, axis=-1, keepdims=True)
    var = jnp.mean(jnp.square(y - mu), axis=-1, keepdims=True)
    return ((y - mu) * lax.rsqrt(var + LN_EPS)) * g + b


def _proj_mix_kernel(*refs, alpha):
    if alpha is not None:
        x_ref, part_p_ref, yc_p_ref, woutc_p_ref, lng_p_ref, lnb_p_ref = refs[:6]
        refs = refs[6:]
    else:
        x_ref, refs = refs[0], refs[1:]
    (mem_ref, w_in_ref, w_tail_ref, sln_g_ref, sln_b_ref, sw_ref, sb_ref, cw_ref, cb_ref, wbd_ref, ba_ref, bx_ref,
     lam_ref, wmem_ref, wout_ref) = refs[:15]
    refs = refs[15:]
    if alpha is not None:
        xo_ref, refs = refs[0], refs[1:]
    (part_ref, q_ref, zc_ref, gl_ref, kvc_ref, ksa_ref, vs_ref, kw_ref, vw_ref,
     conv_buf, h_carry, memk, memv) = refs
    s = pl.program_id(1)
    ts = x_ref.shape[1]
    if alpha is not None:
        x_new = _layer_output(x_ref[0], part_p_ref[0], yc_p_ref[0], woutc_p_ref[...], lng_p_ref[...],
                              lnb_p_ref[...], alpha)
        xo_ref[0] = x_new
        xb = x_new.astype(BF16)
    else:
        xb = x_ref[0].astype(BF16)

    @pl.when(s == 0)
    def _():
        conv_buf[0:8, :] = jnp.zeros((8, LRU_WIDTH), F32)
        h_carry[...] = jnp.zeros_like(h_carry)
        kv = _dot(mem_ref[0].astype(BF16), wmem_ref[...])
        memk[...] = kv[:, :MEM_WIDTH].astype(BF16)
        memv[...] = kv[:, MEM_WIDTH:].astype(BF16)

    pa = _dot(xb, w_in_ref[:, COL_A:COL_A + 3 * A_WIDTH])
    u = _gelu(pa[:, 0:A_WIDTH])
    v = _gelu(pa[:, A_WIDTH:2 * A_WIDTH])
    za = pa[:, 2 * A_WIDTH:3 * A_WIDTH]
    row = lax.broadcasted_iota(jnp.int32, (CHUNK, CHUNK), 0)
    col = lax.broadcasted_iota(jnp.int32, (CHUNK, CHUNK), 1)
    causal = col <= row
    sv_groups = []
    for g in range(A_GROUPS):
        vg = v[:, g * CHUNK:(g + 1) * CHUNK]
        mu = jnp.mean(vg, axis=-1, keepdims=True)
        var = jnp.mean(jnp.square(vg - mu), axis=-1, keepdims=True)
        vn = (vg - mu) * lax.rsqrt(var + LN_EPS)
        vn = vn * sln_g_ref[:, g * CHUNK:(g + 1) * CHUNK] + sln_b_ref[:, g * CHUNK:(g + 1) * CHUNK]
        vnb = vn.astype(BF16)
        wg = jnp.where(causal, sw_ref[g], 0.0).astype(BF16)
        bias = sb_ref[g]
        chunks = []
        for c in range(ts // CHUNK):
            chunks.append(_dot(wg, vnb[c * CHUNK:(c + 1) * CHUNK]) + bias)
        sv_groups.append(jnp.concatenate(chunks, axis=0))
    sv = jnp.concatenate(sv_groups, axis=1)
    ya = (u * sv) * _silu(za)

    pb = _dot(xb, w_in_ref[:, COL_B:COL_B + 2 * LRU_WIDTH])
    xbv = pb[:, :LRU_WIDTH]
    zb = pb[:, LRU_WIDTH:]
    conv_buf[8:8 + ts, :] = xbv
    xc = cb_ref[...] + conv_buf[5:5 + ts, :] * cw_ref[0:1, :]
    xc = xc + conv_buf[6:6 + ts, :] * cw_ref[1:2, :]
    xc = xc + conv_buf[7:7 + ts, :] * cw_ref[2:3, :]
    xc = xc + xbv * cw_ref[3:4, :]
    conv_buf[0:8, :] = xbv[ts - 8:ts, :]
    xcb = xc.astype(BF16)
    half = LRU_WIDTH // 2
    g0 = _dot(xcb[:, :half], wbd_ref[0])
    g1 = _dot(xcb[:, half:], wbd_ref[1])
    r = _sigmoid(jnp.concatenate([g0[:, :half], g1[:, :half]], axis=1) + ba_ref[...])
    ig = _sigmoid(jnp.concatenate([g0[:, half:], g1[:, half:]], axis=1) + bx_ref[...])
    nlam = -lam_ref[...]
    softplus = jnp.maximum(nlam, 0.0) + jnp.log1p(jnp.exp(-jnp.abs(nlam)))
    log_a = (-C_LRU * r) * softplus
    a_cum = jnp.exp(log_a)
    th = jnp.tanh(log_a)
    b_cum = jnp.sqrt((-2.0 * th) / (1.0 - th)) * (ig * xc)
    sub = SUBLANES
    a_cum = a_cum.reshape(ts // sub, sub, LRU_WIDTH)
    b_cum = b_cum.reshape(ts // sub, sub, LRU_WIDTH)
    row_in_block = lax.broadcasted_iota(jnp.int32, (1, sub, 1), 1)
    d = 1
    while d < sub:
        keep = row_in_block >= d
        a_sh = jnp.where(keep, pltpu.roll(a_cum, d, axis=1), 1.0)
        b_sh = jnp.where(keep, pltpu.roll(b_cum, d, axis=1), 0.0)
        b_cum = b_cum + a_cum * b_sh
        a_cum = a_cum * a_sh
        d *= 2
    carry = h_carry[...]
    blocks = []
    for blk in range(ts // sub):
        hb = b_cum[blk] + a_cum[blk] * carry
        carry = hb[sub - 1:sub, :]
        blocks.append(hb)
    h = jnp.concatenate(blocks, axis=0)
    h_carry[...] = carry
    yb = h * _silu(zb)

    pd = _dot(xb, w_tail_ref[:, TAIL_D:TAIL_D + 2 * MEM_WIDTH])
    qd = pd[:, :MEM_WIDTH] * (HEAD_DIM ** -0.5)
    zd = pd[:, MEM_WIDTH:]
    head_of_lane = lax.broadcasted_iota(jnp.int32, (1, MEM_WIDTH), 1) // HEAD_DIM
    od = jnp.zeros((ts, MEM_WIDTH), F32)
    for hd in range(MEM_HEADS):
        hm = head_of_lane == hd
        qh = jnp.where(hm, qd, 0.0).astype(BF16)
        sc = _dot_nt(qh, memk[...])
        m = jnp.max(sc, axis=-1, keepdims=True)
        p = jnp.exp(sc - m)
        l = jnp.sum(p, axis=-1, keepdims=True)
        pv = _dot(p.astype(BF16), memv[...])
        od = od + jnp.where(hm, pv * (1.0 / l), 0.0)
    yd = od * _silu(zd)

    y_abd = jnp.concatenate([ya, yb, yd], axis=1).astype(BF16)
    part_ref[0] = _dot(y_abd, wout_ref[...])

    pc = _dot(xb, w_in_ref[:, COL_C:COL_C + 2 * NSA_WIDTH + 6 * KV_WIDTH])
    q_ref[0] = (pc[:, :NSA_WIDTH] * Q_SCALE).astype(BF16)
    zc_ref[0] = pc[:, NSA_WIDTH:2 * NSA_WIDTH]
    gl_ref[0] = _dot(xb, w_tail_ref[:, TAIL_G:TAIL_G + NSA_KV * LANES])
    base = 2 * NSA_WIDTH
    kpos = s * ts + lax.broadcasted_iota(jnp.int32, (ts, SEL_LANES), 0)
    blk = lax.broadcasted_iota(jnp.int32, (ts, SEL_LANES), 1)
    sel_rows = jnp.where((kpos // SLC_BLOCK) == blk, -NEG, 0.0).astype(BF16)
    ones_col = jnp.where(lax.broadcasted_iota(jnp.int32, (ts, HEAD_DIM), 1) == 0, 1.0, 0.0).astype(BF16)
    for g in range(NSA_KV):
        lo = g * HEAD_DIM
        kvc_ref[0, 0, g] = pc[:, base + lo:base + lo + HEAD_DIM].astype(BF16)
        kvc_ref[0, 1, g] = pc[:, base + KV_WIDTH + lo:base + KV_WIDTH + lo + HEAD_DIM].astype(BF16)
        ksa_ref[0, g, :, 0:SEL_LANES] = sel_rows
        ksa_ref[0, g, :, SEL_LANES:SEL_LANES + HEAD_DIM] = (
            pc[:, base + 2 * KV_WIDTH + lo:base + 2 * KV_WIDTH + lo + HEAD_DIM].astype(BF16))
        vs_ref[0, g, :, 0:HEAD_DIM] = (
            pc[:, base + 3 * KV_WIDTH + lo:base + 3 * KV_WIDTH + lo + HEAD_DIM].astype(BF16))
        vs_ref[0, g, :, HEAD_DIM:2 * HEAD_DIM] = ones_col
        kw_ref[0, g] = pc[:, base + 4 * KV_WIDTH + lo:base + 4 * KV_WIDTH + lo + HEAD_DIM].astype(BF16)
        vw_ref[0, g, :, 0:HEAD_DIM] = (
            pc[:, base + 5 * KV_WIDTH + lo:base + 5 * KV_WIDTH + lo + HEAD_DIM].astype(BF16))
        vw_ref[0, g, :, HEAD_DIM:2 * HEAD_DIM] = ones_col


def _proj_mix(x, prev, alpha, layer, mem, w_in_b, w_tail, sln_g, sln_b, sw, sb, cw, cb, wbd, ba, bx, lam, wmem,
              wout_abd):
    bsz, seq, dm = x.shape
    ts = TS_PROJ
    n_mem = mem.shape[1]
    tok = lambda b, s: (b, s, 0)
    stacked = (w_in_b, w_tail, wbd, wmem, wout_abd)

    def whole(a):
        if any(a is st for st in stacked):
            return pl.BlockSpec((None,) + a.shape[1:], lambda b, s: (layer,) + (0,) * (a.ndim - 1),
                                pipeline_mode=pl.Buffered(1))
        return pl.BlockSpec(a.shape, lambda b, s: (0,) * a.ndim, pipeline_mode=pl.Buffered(1))

    consts = (w_in_b, w_tail, sln_g, sln_b, sw, sb, cw, cb, wbd, ba, bx, lam, wmem, wout_abd)
    args = [x]
    in_specs = [pl.BlockSpec((1, ts, dm), tok)]
    if prev is not None:
        part_p, yc_p, wout_c_p, ln_g_p, ln_b_p = prev
        args += [part_p, yc_p, wout_c_p, ln_g_p, ln_b_p]
        in_specs += [pl.BlockSpec((1, ts, dm), tok), pl.BlockSpec((1, ts, NSA_WIDTH), tok),
                     whole(wout_c_p), whole(ln_g_p), whole(ln_b_p)]
    args += [mem, *consts]
    in_specs += [pl.BlockSpec((1, n_mem, dm), lambda b, s: (b, 0, 0), pipeline_mode=pl.Buffered(1))]
    in_specs += [whole(a) for a in consts]
    kv_spec = pl.BlockSpec((1, NSA_KV, ts, HEAD_DIM), lambda b, s: (b, 0, s, 0))
    va_spec = pl.BlockSpec((1, NSA_KV, ts, 2 * HEAD_DIM), lambda b, s: (b, 0, s, 0))
    out_specs = [
        pl.BlockSpec((1, ts, dm), lambda b, s: (b, s, 0)),
        pl.BlockSpec((1, ts, NSA_WIDTH), lambda b, s: (b, s, 0)),
        pl.BlockSpec((1, ts, NSA_WIDTH), lambda b, s: (b, s, 0)),
        pl.BlockSpec((1, ts, NSA_KV * LANES), lambda b, s: (b, s, 0)),
        pl.BlockSpec((1, 2, NSA_KV, ts, HEAD_DIM), lambda b, s: (b, 0, 0, s, 0)),
        pl.BlockSpec((1, NSA_KV, ts, SEL_LANES + HEAD_DIM), lambda b, s: (b, 0, s, 0)),
        va_spec, kv_spec, va_spec,
    ]
    out_shape = [
        jax.ShapeDtypeStruct((bsz, seq, dm), F32),
        jax.ShapeDtypeStruct((bsz, seq, NSA_WIDTH), BF16),
        jax.ShapeDtypeStruct((bsz, seq, NSA_WIDTH), F32),
        jax.ShapeDtypeStruct((bsz, seq, NSA_KV * LANES), F32),
        jax.ShapeDtypeStruct((bsz, 2, NSA_KV, seq, HEAD_DIM), BF16),
        jax.ShapeDtypeStruct((bsz, NSA_KV, seq, SEL_LANES + HEAD_DIM), BF16),
        jax.ShapeDtypeStruct((bsz, NSA_KV, seq, 2 * HEAD_DIM), BF16),
        jax.ShapeDtypeStruct((bsz, NSA_KV, seq, HEAD_DIM), BF16),
        jax.ShapeDtypeStruct((bsz, NSA_KV, seq, 2 * HEAD_DIM), BF16),
    ]
    if prev is not None:
        out_specs = [pl.BlockSpec((1, ts, dm), tok)] + out_specs
        out_shape = [jax.ShapeDtypeStruct((bsz, seq, dm), F32)] + out_shape
    return pl.pallas_call(
        functools.partial(_proj_mix_kernel, alpha=alpha if prev is not None else None),
        grid=(bsz, seq // ts),
        in_specs=in_specs,
        out_specs=out_specs,
        out_shape=out_shape,
        scratch_shapes=[
            pltpu.VMEM((ts + 8, LRU_WIDTH), F32),
            pltpu.VMEM((1, LRU_WIDTH), F32),
            pltpu.VMEM((n_mem, MEM_WIDTH), BF16),
            pltpu.VMEM((n_mem, MEM_WIDTH), BF16),
        ],
        compiler_params=pltpu.CompilerParams(
            dimension_semantics=("arbitrary", "arbitrary"), vmem_limit_bytes=VMEM_LIMIT),
        name="proj_mix",
    )(*args)


def _compress_kernel(fc_ref, pos_ref, w1_ref, w2_ref, out_ref):
    fc = fc_ref[0, 0, 0].astype(F32)
    half = (CMP_LEN // 2) * HEAD_DIM
    pos = pos_ref[0]
    fa = (fc + pos[:, :half]).astype(BF16)
    fb = (fc + pos[:, half:]).astype(BF16)
    ha = _dot(fa, w1_ref[0, :half, :])
    hb = _dot(fb, w1_ref[0, half:, :])
    hb_next = jnp.concatenate([hb[1:], hb[:1]], axis=0)
    hid = _gelu(ha + hb_next)
    out_ref[0, 0, 0] = _dot(hid.astype(BF16), w2_ref[0]).astype(BF16)


def _compress(fc, pos_flat, w1, w2):
    bsz, _, _, n_chunk, width = fc.shape
    return pl.pallas_call(
        _compress_kernel,
        grid=(bsz, 2, NSA_KV),
        in_specs=[
            pl.BlockSpec((1, 1, 1, n_chunk, width), lambda b, k, g: (b, k, g, 0, 0)),
            pl.BlockSpec((1, 1, CMP_LEN * HEAD_DIM), lambda b, k, g: (k, 0, 0)),
            pl.BlockSpec((1, CMP_LEN * HEAD_DIM, CMP_HIDDEN), lambda b, k, g: (k, 0, 0)),
            pl.BlockSpec((1, CMP_HIDDEN, HEAD_DIM), lambda b, k, g: (k, 0, 0)),
        ],
        out_specs=pl.BlockSpec((1, 1, 1, n_chunk, HEAD_DIM), lambda b, k, g: (b, k, g, 0, 0)),
        out_shape=jax.ShapeDtypeStruct((bsz, 2, NSA_KV, n_chunk, HEAD_DIM), BF16),
        compiler_params=pltpu.CompilerParams(
            dimension_semantics=("arbitrary", "arbitrary", "arbitrary"), vmem_limit_bytes=VMEM_LIMIT),
        name="compress",
    )(fc, pos_flat, w1, w2)


def _nsa_kernel(q_ref, zc_ref, gl_ref, kcv_ref, ksa_ref, vs_ref, kw_ref, vw_ref, ovt_ref, bias_ref, y_ref, s_buf,
                m_ref, acc_ref, *, n_sel):
    qi = pl.program_id(2)
    tq = q_ref.shape[1]
    rows = NSA_GROUP * tq
    s0 = qi * tq
    q = q_ref[0]
    q_heads = [q[:, h * HEAD_DIM:(h + 1) * HEAD_DIM] for h in range(NSA_GROUP)]
    qs = jnp.concatenate(q_heads, axis=0)
    t_q = s0 + lax.broadcasted_iota(jnp.int32, (tq, 1), 0)

    blk = lax.broadcasted_iota(jnp.int32, (SEL_LANES, 1), 0)
    blk_f = blk.astype(F32)

    def compressed_branch():
        kc = kcv_ref[0, 0, 0]
        vc = kcv_ref[0, 1, 0]
        n_cmp = kc.shape[0]
        cmp_end = lax.broadcasted_iota(jnp.int32, (1, n_cmp), 1) * CMP_STRIDE + (CMP_LEN - 1)
        valid_c = cmp_end <= t_q
        any_valid = t_q >= CMP_LEN - 1
        o_c = []
        ps = None
        for h in range(NSA_GROUP):
            sc = jnp.where(valid_c, _dot_nt(q_heads[h], kc), NEG)
            m = jnp.max(sc, axis=-1, keepdims=True)
            p = jnp.exp2(sc - m)
            l = jnp.sum(p, axis=-1, keepdims=True)
            pn = p * jnp.where(any_valid, 1.0 / l, 0.0)
            o_c.append(_dot(pn.astype(BF16), vc))
            ps = pn if ps is None else ps + pn
        ps_hi = ps.astype(BF16)
        ps_lo = (ps - ps_hi.astype(F32)).astype(BF16)
        imp = _dot_nt(ovt_ref[...], ps_hi) + _dot_nt(ovt_ref[...], ps_lo)
        cur = (s0 + lax.broadcasted_iota(jnp.int32, (1, tq), 1)) // SLC_BLOCK
        future = blk > cur
        forced = (blk == 0) | (blk == cur) | (blk == cur - 1)
        return jnp.where(future, -1.0, imp + jnp.where(forced, FORCE_BONUS, 0.0)), tuple(o_c)

    def knock_out(work_i, rounds):
        for _ in range(rounds):
            mx = jnp.max(work_i, axis=0, keepdims=True)
            idx = jnp.min(jnp.where(work_i == mx, blk_f, float(SEL_LANES)), axis=0, keepdims=True)
            work_i = jnp.where(blk_f == idx, -jnp.inf, work_i)
        return work_i

    n_tiles = ksa_ref.shape[2] // TK

    def tile_start(kt):
        return pl.multiple_of(jnp.clip(kt, 0, n_tiles - 1) * TK, TK)

    one = 1 + jnp.minimum(qi, 0)

    def run_tiles(q_op, k_ref, v_ref, kt0, stride, n_plain, n_masked, bias_fn, side_fn=None, side=None):
        def tile(ref, i):
            return ref[0, 0, pl.ds(tile_start(kt0 + stride * i), TK), :]

        def step(i, slot, state, prefetch, masked, pos=0):
            m_i, acc, side_i = state
            s = s_buf[slot]
            if prefetch:
                s_buf[1 - slot] = _dot_nt(q_op, tile(k_ref, i + 1))
            if masked:
                s = (s.reshape(NSA_GROUP, tq, TK) + bias_fn(pos)[None]).reshape(rows, TK)
            m_new = jnp.maximum(m_i, jnp.broadcast_to(jnp.max(s, axis=-1, keepdims=True), m_i.shape))
            p_t = jnp.exp2(s - jnp.concatenate([m_new] * (TK // LANES), axis=1)).astype(BF16)
            acc = jnp.exp2(m_i - m_new) * acc + _dot(p_t, tile(v_ref, i))
            return m_new, acc, (side_fn(pos, side_i) if side_fn is not None else side_i)

        def group(width, first, masked, prefetch_last=True, slot0=0):
            def body(j, side_i):
                state = (m_ref[...], acc_ref[...], side_i)
                for i in range(width):
                    slot = slot0 if width == 1 else i % 2
                    state = step(first + width * j + i, slot, state, prefetch_last or i + 1 < width, masked, i)
                m_ref[...] = state[0]
                acc_ref[...] = state[1]
                return state[2]
            return body

        s_buf[0] = _dot_nt(q_op, tile(k_ref, 0))
        m_ref[...] = jnp.full(m_ref.shape, NEG, F32)
        acc_ref[...] = jnp.zeros(acc_ref.shape, F32)
        if isinstance(n_plain, int):
            assert n_plain == 0 and n_masked == 2
            side = lax.fori_loop(0, one, group(2, 0, True, prefetch_last=False), side)
        else:
            assert n_masked == 1
            n8 = n_plain // 8
            side = lax.fori_loop(0, n8, group(8, 0, False), side)
            side = lax.fori_loop(0, (n_plain % 8) // 4, group(4, 8 * n8, False), side)
            n4 = n_plain // 4
            side = lax.fori_loop(0, (n_plain % 4) // 2, group(2, 4 * n4, False), side)
            side = lax.fori_loop(0, n_plain % 2, group(1, n_plain - 1, False), side)
            side = lax.fori_loop(0, one, group(1, n_plain, True, prefetch_last=False, slot0=n_plain % 2), side)
        acc = acc_ref[...]
        return acc[:, :HEAD_DIM] * (1.0 / acc[:, HEAD_DIM:HEAD_DIM + 1]), side

    assert tq == TK == WINDOW
    older = jnp.where(qi > 0, bias_ref[1], NEG)

    assert n_sel % 2 == 0
    work, o_c = compressed_branch()
    o_w, work = run_tiles(qs, kw_ref, vw_ref, qi, -1, 0, 2, lambda i: bias_ref[0] if i == 0 else older,
                          side_fn=lambda pos, w: knock_out(w, n_sel // 2), side=work)
    sel_t = jnp.where(work == -jnp.inf, 0.0, -1.0)
    sel_rows = jnp.concatenate([sel_t.T.astype(BF16)] * NSA_GROUP, axis=0)
    q_aug = jnp.concatenate([sel_rows, qs], axis=1)

    o_s, _ = run_tiles(q_aug, ksa_ref, vs_ref, 0, 1, qi, 1, lambda i: bias_ref[0])

    gates = _sigmoid(gl_ref[0])
    heads = []
    for h in range(NSA_GROUP):
        y_h = o_c[h] * gates[:, 3 * h:3 * h + 1]
        y_h = y_h + o_s[h * tq:(h + 1) * tq] * gates[:, 3 * h + 1:3 * h + 2]
        y_h = y_h + o_w[h * tq:(h + 1) * tq] * gates[:, 3 * h + 2:3 * h + 3]
        heads.append(y_h)
    y = jnp.concatenate(heads, axis=1)
    y_ref[0] = (y * _silu(zc_ref[0])).astype(BF16)


def _nsa(q, zc, gl, kcv, ksa, vs, kw, vw, overlap, tile_bias):
    bsz, seq, _ = q.shape
    n_cmp = kcv.shape[3]
    gw = NSA_GROUP * HEAD_DIM
    n_sel = min(N_SELECT, seq // SLC_BLOCK)
    per_bg = lambda b, g, i: (b, g, 0, 0)
    return pl.pallas_call(
        functools.partial(_nsa_kernel, n_sel=n_sel),
        grid=(bsz, NSA_KV, seq // TQ),
        in_specs=[
            pl.BlockSpec((1, TQ, gw), lambda b, g, i: (b, i, g)),
            pl.BlockSpec((1, TQ, gw), lambda b, g, i: (b, i, g)),
            pl.BlockSpec((1, TQ, LANES), lambda b, g, i: (b, i, g)),
            pl.BlockSpec((1, 2, 1, n_cmp, HEAD_DIM), lambda b, g, i: (b, 0, g, 0, 0)),
            pl.BlockSpec((1, 1, seq, SEL_LANES + HEAD_DIM), per_bg, pipeline_mode=pl.Buffered(1)),
            pl.BlockSpec((1, 1, seq, 2 * HEAD_DIM), per_bg, pipeline_mode=pl.Buffered(1)),
            pl.BlockSpec((1, 1, seq, HEAD_DIM), per_bg, pipeline_mode=pl.Buffered(1)),
            pl.BlockSpec((1, 1, seq, 2 * HEAD_DIM), per_bg, pipeline_mode=pl.Buffered(1)),
            pl.BlockSpec(overlap.shape, lambda b, g, i: (0, 0), pipeline_mode=pl.Buffered(1)),
            pl.BlockSpec(tile_bias.shape, lambda b, g, i: (0, 0, 0), pipeline_mode=pl.Buffered(1)),
        ],
        out_specs=pl.BlockSpec((1, TQ, gw), lambda b, g, i: (b, i, g)),
        out_shape=jax.ShapeDtypeStruct((bsz, seq, NSA_WIDTH), BF16),
        scratch_shapes=[pltpu.VMEM((2, NSA_GROUP * TQ, TK), F32),
                        pltpu.VMEM((NSA_GROUP * TQ, LANES), F32),
                        pltpu.VMEM((NSA_GROUP * TQ, 2 * HEAD_DIM), F32)],
        compiler_params=pltpu.CompilerParams(
            dimension_semantics=("arbitrary", "arbitrary", "arbitrary"), vmem_limit_bytes=VMEM_LIMIT),
        name="nsa",
    )(q, zc, gl, kcv, ksa, vs, kw, vw, overlap, tile_bias)


def _final_kernel(x_ref, part_ref, yc_ref, wout_ref, g_ref, b_ref, o_ref, *, alpha):
    o_ref[0] = _layer_output(x_ref[0], part_ref[0], yc_ref[0], wout_ref[...], g_ref[...], b_ref[...], alpha)


def _final(x, part, yc, wout_c, ln_g, ln_b, alpha):
    bsz, seq, dm = x.shape
    ts = TS_PROJ
    tok = lambda b, s: (b, s, 0)
    return pl.pallas_call(
        functools.partial(_final_kernel, alpha=alpha),
        grid=(bsz, seq // ts),
        in_specs=[
            pl.BlockSpec((1, ts, dm), tok),
            pl.BlockSpec((1, ts, dm), tok),
            pl.BlockSpec((1, ts, NSA_WIDTH), tok),
            pl.BlockSpec(wout_c.shape, lambda b, s: (0, 0)),
            pl.BlockSpec(ln_g.shape, lambda b, s: (0, 0)),
            pl.BlockSpec(ln_b.shape, lambda b, s: (0, 0)),
        ],
        out_specs=pl.BlockSpec((1, ts, dm), tok),
        out_shape=jax.ShapeDtypeStruct((bsz, seq, dm), F32),
        compiler_params=pltpu.CompilerParams(
            dimension_semantics=("arbitrary", "arbitrary"), vmem_limit_bytes=VMEM_LIMIT),
        name="final",
    )(x, part, yc, wout_c, ln_g, ln_b)


def _pack_w_tail(w_in):
    per_group = N_GATES // NSA_KV
    gates = w_in[:, :, COL_TAIL:COL_TAIL + N_GATES]
    blocks = []
    for g in range(NSA_KV):
        blk = gates[:, :, g * per_group:(g + 1) * per_group]
        blocks.append(jnp.pad(blk, ((0, 0), (0, 0), (0, LANES - per_group))))
    blocks.append(w_in[:, :, COL_TAIL + N_GATES:])
    return jnp.concatenate(blocks, axis=-1).astype(BF16)


def _block_diag_gates(wa, wx):
    depth = wa.shape[0]
    per_half = LRU_BLOCKS // 2
    half = per_half * LRU_BLOCK
    halves = []
    for hf in range(2):
        mats = []
        for w in (wa, wx):
            m = jnp.zeros((depth, half, half), F32)
            for j in range(per_half):
                m = m.at[:, j * LRU_BLOCK:(j + 1) * LRU_BLOCK, j * LRU_BLOCK:(j + 1) * LRU_BLOCK].set(
                    w[:, hf * per_half + j])
            mats.append(m)
        halves.append(jnp.concatenate(mats, axis=-1))
    return jnp.stack(halves, axis=1).astype(BF16)


def _overlap_matrix(n_cmp_pad, n_cmp, n_slc):
    ci = np.arange(n_cmp_pad)[:, None] * CMP_STRIDE
    sj = np.arange(SEL_LANES)[None, :] * SLC_BLOCK
    ov = (ci < sj + SLC_BLOCK) & (ci + CMP_LEN > sj)
    ov &= (np.arange(n_cmp_pad)[:, None] < n_cmp) & (np.arange(SEL_LANES)[None, :] < n_slc)
    return jnp.asarray(ov.T.astype(np.float32), dtype=BF16)


def _tile_bias():
    r = np.arange(TQ)[:, None]
    k = np.arange(TK)[None, :]
    diag = np.where(k <= r, 0.0, NEG)
    older = np.where(k > r, 0.0, NEG)
    return jnp.asarray(np.stack([diag, older]), dtype=F32)


def kernel(x, mem, w_in, sgu_ln_g, sgu_ln_b, sgu_w, sgu_b, conv_w, conv_b, lru_wa, lru_ba, lru_wx, lru_bx,
           lru_lambda, cmp_pos, cmp_w1, cmp_w2, w_mem_kv, w_out, ln_g, ln_b):
    depth = w_in.shape[0]
    bsz, seq, dm = x.shape
    assert TQ == TK and WINDOW <= TK
    assert seq % TK == 0 and seq // SLC_BLOCK <= SEL_LANES
    alpha = (2 * depth) ** 0.25
    n_chunk = seq // CMP_STRIDE
    n_cmp = (seq - CMP_LEN) // CMP_STRIDE + 1

    w_in_b = w_in.astype(BF16)
    w_tail = _pack_w_tail(w_in)
    wbd = _block_diag_gates(lru_wa, lru_wx)
    wmem = w_mem_kv.astype(BF16)
    c0 = A_WIDTH + LRU_WIDTH
    wout_abd = jnp.concatenate([w_out[:, :c0], w_out[:, c0 + NSA_WIDTH:]], axis=1).astype(BF16)
    wout_c = w_out[:, c0:c0 + NSA_WIDTH].astype(BF16)
    w1 = cmp_w1.astype(BF16)
    w2 = cmp_w2.astype(BF16)
    pos_flat = cmp_pos.reshape(depth, 2, 1, CMP_LEN * HEAD_DIM)
    overlap = _overlap_matrix(n_chunk, n_cmp, seq // SLC_BLOCK)
    tile_bias = _tile_bias()
    row = lambda a: a[:, None, :]

    prev = None
    for l in range(depth):
        outs = _proj_mix(
            x, prev, alpha, l, mem, w_in_b, w_tail, row(sgu_ln_g)[l], row(sgu_ln_b)[l], sgu_w[l],
            sgu_b[l][:, :, None], conv_w[l], row(conv_b)[l], wbd, row(lru_ba)[l], row(lru_bx)[l],
            row(lru_lambda)[l], wmem, wout_abd)
        if prev is not None:
            x, outs = outs[0], outs[1:]
        part, q, zc, gl, kvc_raw, ksa, vs, kw, vw = outs
        fc = kvc_raw.reshape(bsz, 2, NSA_KV, n_chunk, CMP_STRIDE * HEAD_DIM)
        kcv = _compress(fc, pos_flat[l], w1[l], w2[l])
        yc = _nsa(q, zc, gl, kcv, ksa, vs, kw, vw, overlap, tile_bias)
        prev = (part, yc, wout_c[l], row(ln_g)[l], row(ln_b)[l])
    return _final(x, *prev, alpha)
```
